```python
import math
import jax, jax.numpy as jnp
from jax import lax
import numpy as np

D_MODEL = 1024
BATCH = 4
SEQ = 4096
DEPTH = 2
DEC_BATCH = 32
DEC_SEQ = 4
PAST_LEN = 16384
PAGE_SIZE = 128

W_BRANCH = 512
N_BRANCH = 3
N_HEADS_A = 4
HEAD_DIM = 64
ROPE_THETA = 10000.0
Q_BLOCK = 128
CHUNK = 128
N_GROUPS_G = 4
GROUP_CH_G = W_BRANCH // N_GROUPS_G
SSM_GROUP = 16
N_GROUPS_S = W_BRANCH // SSM_GROUP
SSM_STATE = 64
IN_COLS = 6 * W_BRANCH + N_BRANCH * D_MODEL
D_FF = -(-8 * D_MODEL // (3 * 256)) * 256
ALPHA = (2 * DEPTH) ** 0.25
BETA = (8 * DEPTH) ** -0.25
LN_EPS = 1e-5
NEG_BIG = -1e30

kernel_name = 'hybrid_diffattn_gmlp_s5_decoder_step'


def layer_norm(x, g, b):
    xf = x.astype(jnp.float32)
    mu = xf.mean(-1, keepdims=True)
    var = jnp.square(xf - mu).mean(-1, keepdims=True)
    return ((xf - mu) * lax.rsqrt(var + LN_EPS)).astype(x.dtype) * g + b


def rms_norm(x, g):
    xf = x.astype(jnp.float32)
    return (xf * lax.rsqrt(jnp.mean(xf * xf, -1, keepdims=True) + LN_EPS)).astype(x.dtype) * g


def rotary(x, pos):
    half = HEAD_DIM // 2
    inv = ROPE_THETA ** (-jnp.arange(half, dtype=jnp.float32) / half)
    ang = pos.astype(jnp.float32)[:, None] * inv[None, :]
    cos = jnp.cos(ang)[:, None, None, :].astype(x.dtype)
    sin = jnp.sin(ang)[:, None, None, :].astype(x.dtype)
    x1, x2 = x[..., :half], x[..., half:]
    return jnp.concatenate([x1 * cos - x2 * sin, x1 * sin + x2 * cos], axis=-1)


def lambda_init(l):
    return 0.8 - 0.6 * math.exp(-0.3 * l)


def diff_attn_prompt(q, k, v, lam):
    b, L = q.shape[:2]
    nb = L // Q_BLOCK
    qb = q.reshape(b, nb, Q_BLOCK, N_HEADS_A, 2, HEAD_DIM).swapaxes(0, 1)
    kpos = jnp.arange(L)
    scale = HEAD_DIM ** -0.5

    def one_block(args):
        qi, i = args
        s = jnp.einsum('bqhcd,bkhcd->bhcqk', qi, k).astype(jnp.float32) * scale
        qpos = i * Q_BLOCK + jnp.arange(Q_BLOCK)
        s = jnp.where(kpos[None, :] <= qpos[:, None], s, NEG_BIG)
        p = jax.nn.softmax(s, axis=-1)
        a = (p[:, :, 0] - lam * p[:, :, 1]).astype(v.dtype)
        return jnp.einsum('bhqk,bkhe->bqhe', a, v)

    o = lax.map(one_block, (qb, jnp.arange(nb)))
    return o.swapaxes(0, 1).reshape(b, L, N_HEADS_A, 2 * HEAD_DIM)


def diff_attn_sample(q, k, v, k_past, v_past, lam):
    n = q.shape[1]
    n_past = k_past.shape[1]
    scale = HEAD_DIM ** -0.5
    s_past = jnp.einsum('bqhcd,bkhcd->bhcqk', q, k_past).astype(jnp.float32) * scale
    s_new = jnp.einsum('bqhcd,bkhcd->bhcqk', q, k).astype(jnp.float32) * scale
    causal = jnp.tril(jnp.ones((n, n), dtype=bool))
    s_new = jnp.where(causal, s_new, NEG_BIG)
    p = jax.nn.softmax(jnp.concatenate([s_past, s_new], axis=-1), axis=-1)
    a = (p[:, :, 0] - lam * p[:, :, 1]).astype(v.dtype)
    return (jnp.einsum('bhqk,bkhe->bqhe', a[..., :n_past], v_past)
            + jnp.einsum('bhqk,bkhe->bqhe', a[..., n_past:], v))


def ssm_discretize(a_re, a_im, log_dt, b_re, b_im):
    f32 = jnp.float32
    dt = jnp.exp(log_dt.astype(f32))[:, None]
    ar, ai = a_re.astype(f32), a_im.astype(f32)
    mag = jnp.exp(dt * ar)
    abar_re = mag * jnp.cos(dt * ai)
    abar_im = mag * jnp.sin(dt * ai)
    nr, ni = abar_re - 1.0, abar_im
    den = ar * ar + ai * ai
    c_re = (nr * ar + ni * ai) / den
    c_im = (ni * ar - nr * ai) / den
    br, bi = b_re.astype(f32), b_im.astype(f32)
    bbar_re = c_re[..., None] * br - c_im[..., None] * bi
    bbar_im = c_re[..., None] * bi + c_im[..., None] * br
    return abar_re, abar_im, bbar_re, bbar_im


def ssm_combine(e1, e2):
    a1r, a1i, b1r, b1i = e1
    a2r, a2i, b2r, b2i = e2
    return (a2r * a1r - a2i * a1i, a2r * a1i + a2i * a1r,
            a2r * b1r - a2i * b1i + b2r, a2r * b1i + a2i * b1r + b2i)


def ssm_branch(u, h0_re, h0_im, P):
    b, L = u.shape[:2]
    uf = u.astype(jnp.float32).reshape(b, L, N_GROUPS_S, SSM_GROUP)
    abr, abi, bbr, bbi = ssm_discretize(P['ssm_a_re'], P['ssm_a_im'], P['ssm_log_dt'],
                                        P['ssm_b_re'], P['ssm_b_im'])
    xr = jnp.einsum('gph,blgh->blgp', bbr, uf)
    xi = jnp.einsum('gph,blgh->blgp', bbi, uf)
    if h0_re is not None:
        hr0, hi0 = h0_re.astype(jnp.float32), h0_im.astype(jnp.float32)
        xr = xr.at[:, 0].add(abr * hr0 - abi * hi0)
        xi = xi.at[:, 0].add(abr * hi0 + abi * hr0)
    ar = jnp.broadcast_to(abr, xr.shape)
    ai = jnp.broadcast_to(abi, xr.shape)
    _, _, hr, hi = lax.associative_scan(ssm_combine, (ar, ai, xr, xi), axis=1)
    c_re = P['ssm_c_re'].astype(jnp.float32)
    c_im = P['ssm_c_im'].astype(jnp.float32)
    y = (jnp.einsum('ghp,blgp->blgh', c_re, hr) - jnp.einsum('ghp,blgp->blgh', c_im, hi)
         + P['ssm_d'].astype(jnp.float32) * uf)
    y = y.reshape(b, L, W_BRANCH).astype(u.dtype)
    a = jax.nn.gelu(y)
    out = a * jax.nn.sigmoid(a @ P['ssm_glu_w'] + P['ssm_glu_b'])
    return out, hr[:, -1], hi[:, -1]


def gmlp_branch(gu, gv, P):
    b, L = gu.shape[:2]
    u = jax.nn.gelu(gu)
    v = layer_norm(jax.nn.gelu(gv), P['gmlp_ln_g'], P['gmlp_ln_b'])
    ws = jnp.tril(P['gmlp_ws'])
    if L >= CHUNK:
        vb = v.reshape(b, L // CHUNK, CHUNK, N_GROUPS_G, GROUP_CH_G)
        mixed = (jnp.einsum('gts,bnsgc->bntgc', ws, vb)
                 + P['gmlp_bs'].T[None, None, :, :, None])
    else:
        vb = v.reshape(b, L, N_GROUPS_G, GROUP_CH_G)
        mixed = (jnp.einsum('gts,bsgc->btgc', ws[:, :L, :L], vb)
                 + P['gmlp_bs'][:, :L].T[None, :, :, None])
    return u * mixed.reshape(b, L, W_BRANCH), v


def mixer_block(x, pos, l, P, past):
    b, L = x.shape[:2]
    h = x @ P['w_in']
    q, k, v, gu, gv, su, gates = jnp.split(h, [W_BRANCH * i for i in range(1, 7)], axis=-1)
    q = rotary(q.reshape(b, L, N_HEADS_A, 2, HEAD_DIM), pos)
    k = rotary(k.reshape(b, L, N_HEADS_A, 2, HEAD_DIM), pos)
    v = v.reshape(b, L, N_HEADS_A, 2 * HEAD_DIM)
    lam_init = lambda_init(l)
    f32 = jnp.float32
    lam = (jnp.exp(jnp.sum(P['lambda_q1'].astype(f32) * P['lambda_k1'].astype(f32)))
           - jnp.exp(jnp.sum(P['lambda_q2'].astype(f32) * P['lambda_k2'].astype(f32)))
           + lam_init)
    if past is None:
        o = diff_attn_prompt(q, k, v, lam)
        h0_re = h0_im = None
    else:
        k_past, v_past, h0_re, h0_im = past
        o = diff_attn_sample(q, k, v, k_past, v_past, lam)
    attn_out = (rms_norm(o, P['attn_norm_g']) * (1.0 - lam_init)).reshape(b, L, W_BRANCH)
    gm_out, v_norm = gmlp_branch(gu, gv, P)
    ssm_out, hr, hi = ssm_branch(su, h0_re, h0_im, P)
    branches = jnp.stack([attn_out, gm_out, ssm_out], axis=2)
    g = jax.nn.sigmoid(gates + P['b_gate']).reshape(b, L, N_BRANCH, D_MODEL)
    proj = jnp.einsum('blnw,nwd->blnd', branches, P['w_branch'])
    merged = jnp.sum(g * proj, axis=2)
    return merged @ P['w_out'], (k, v, v_norm, hr, hi)


def trunk_layer(x, pos, l, P, past):
    mix, new_state = mixer_block(x, pos, l, P, past)
    x = layer_norm(ALPHA * x + mix, P['ln1_g'], P['ln1_b'])
    gate, up = jnp.split(x @ P['w_ffn_in'], 2, axis=-1)
    f = (jax.nn.silu(gate) * up) @ P['w_ffn_out']
    x = layer_norm(ALPHA * x + f, P['ln2_g'], P['ln2_b'])
    return x, new_state


def setup_inputs(seed: int = 0) -> dict:
    key = jax.random.key(seed)
    ks = jax.random.split(key, 40)
    f32 = jnp.float32
    n_pages = PAST_LEN // PAGE_SIZE
    n_used = DEC_BATCH * n_pages
    n_pool = n_used + (n_used + 3) // 4

    def nrm(k, shape, s):
        return jax.random.normal(k, shape, f32) * s

    page_table = jax.random.permutation(ks[4], n_pool)[:n_used].reshape(DEC_BATCH, n_pages).astype(jnp.int32)
    a_im = jnp.pi * jnp.arange(SSM_STATE, dtype=f32)[None, None, :] + nrm(ks[15], (DEPTH, N_GROUPS_S, SSM_STATE), 0.01)
    return {
        'x_prompt': nrm(ks[0], (BATCH, SEQ, D_MODEL), 1.0),
        'x_sample': nrm(ks[1], (DEC_BATCH, DEC_SEQ, D_MODEL), 1.0),
        'cache_k': nrm(ks[2], (DEPTH, n_pool, PAGE_SIZE, N_HEADS_A, 2 * HEAD_DIM), 1.0),
        'cache_v': nrm(ks[3], (DEPTH, n_pool, PAGE_SIZE, N_HEADS_A, 2 * HEAD_DIM), 1.0),
        'page_table': page_table,
        'state_ssm_re': nrm(ks[5], (DEPTH, DEC_BATCH, N_GROUPS_S, SSM_STATE), 0.5),
        'state_ssm_im': nrm(ks[6], (DEPTH, DEC_BATCH, N_GROUPS_S, SSM_STATE), 0.5),
        'w_in': nrm(ks[7], (DEPTH, D_MODEL, IN_COLS), D_MODEL ** -0.5),
        'b_gate': nrm(ks[8], (DEPTH, N_BRANCH * D_MODEL), 0.01),
        'lambda_q1': nrm(ks[9], (DEPTH, HEAD_DIM), 0.1),
        'lambda_k1': nrm(ks[10], (DEPTH, HEAD_DIM), 0.1),
        'lambda_q2': nrm(ks[11], (DEPTH, HEAD_DIM), 0.1),
        'lambda_k2': nrm(ks[12], (DEPTH, HEAD_DIM), 0.1),
        'attn_norm_g': 1.0 + nrm(ks[13], (DEPTH, 2 * HEAD_DIM), 0.01),
        'gmlp_ln_g': 1.0 + nrm(ks[16], (DEPTH, W_BRANCH), 0.01),
        'gmlp_ln_b': nrm(ks[17], (DEPTH, W_BRANCH), 0.01),
        'gmlp_ws': nrm(ks[18], (DEPTH, N_GROUPS_G, CHUNK, CHUNK), CHUNK ** -0.5),
        'gmlp_bs': 1.0 + nrm(ks[19], (DEPTH, N_GROUPS_G, CHUNK), 0.01),
        'ssm_a_re': -0.5 + nrm(ks[14], (DEPTH, N_GROUPS_S, SSM_STATE), 0.01),
        'ssm_a_im': a_im,
        'ssm_log_dt': jax.random.uniform(ks[20], (DEPTH, N_GROUPS_S), f32, math.log(1e-3), math.log(1e-1)),
        'ssm_b_re': nrm(ks[21], (DEPTH, N_GROUPS_S, SSM_STATE, SSM_GROUP), (2 * SSM_GROUP) ** -0.5),
        'ssm_b_im': nrm(ks[22], (DEPTH, N_GROUPS_S, SSM_STATE, SSM_GROUP), (2 * SSM_GROUP) ** -0.5),
        'ssm_c_re': nrm(ks[23], (DEPTH, N_GROUPS_S, SSM_GROUP, SSM_STATE), SSM_STATE ** -0.5),
        'ssm_c_im': nrm(ks[24], (DEPTH, N_GROUPS_S, SSM_GROUP, SSM_STATE), SSM_STATE ** -0.5),
        'ssm_d': nrm(ks[25], (DEPTH, N_GROUPS_S, SSM_GROUP), 0.5),
        'ssm_glu_w': nrm(ks[26], (DEPTH, W_BRANCH, W_BRANCH), W_BRANCH ** -0.5),
        'ssm_glu_b': nrm(ks[27], (DEPTH, W_BRANCH), 0.01),
        'w_branch': nrm(ks[28], (DEPTH, N_BRANCH, W_BRANCH, D_MODEL), W_BRANCH ** -0.5),
        'w_out': nrm(ks[29], (DEPTH, D_MODEL, D_MODEL), BETA * D_MODEL ** -0.5),
        'ln1_g': 1.0 + nrm(ks[30], (DEPTH, D_MODEL), 0.01),
        'ln1_b': nrm(ks[31], (DEPTH, D_MODEL), 0.01),
        'ln2_g': 1.0 + nrm(ks[32], (DEPTH, D_MODEL), 0.01),
        'ln2_b': nrm(ks[33], (DEPTH, D_MODEL), 0.01),
        'w_ffn_in': nrm(ks[34], (DEPTH, D_MODEL, 2 * D_FF), D_MODEL ** -0.5),
        'w_ffn_out': nrm(ks[35], (DEPTH, D_FF, D_MODEL), BETA * D_FF ** -0.5),
    }


def reference(x_prompt, x_sample, cache_k, cache_v, page_table, state_ssm_re, state_ssm_im,
              w_in, b_gate, lambda_q1, lambda_k1, lambda_q2, lambda_k2, attn_norm_g,
              gmlp_ln_g, gmlp_ln_b, gmlp_ws, gmlp_bs,
              ssm_a_re, ssm_a_im, ssm_log_dt, ssm_b_re, ssm_b_im, ssm_c_re, ssm_c_im, ssm_d,
              ssm_glu_w, ssm_glu_b, w_branch, w_out, ln1_g, ln1_b, ln2_g, ln2_b,
              w_ffn_in, w_ffn_out):
    bp, Lp = x_prompt.shape[:2]
    bs_, Ls = x_sample.shape[:2]
    n_pages = page_table.shape[1]
    n_past = n_pages * PAGE_SIZE
    pos_p = jnp.arange(Lp)
    pos_s = n_past + jnp.arange(Ls)
    yp, ys = x_prompt, x_sample
    kp_l, vp_l, ks_l, vs_l = [], [], [], []
    hrp_l, hip_l, hrs_l, his_l, gvs_l = [], [], [], [], []
    for l in range(DEPTH):
        P = {
            'w_in': w_in[l], 'b_gate': b_gate[l],
            'lambda_q1': lambda_q1[l], 'lambda_k1': lambda_k1[l],
            'lambda_q2': lambda_q2[l], 'lambda_k2': lambda_k2[l],
            'attn_norm_g': attn_norm_g[l],
            'gmlp_ln_g': gmlp_ln_g[l], 'gmlp_ln_b': gmlp_ln_b[l],
            'gmlp_ws': gmlp_ws[l], 'gmlp_bs': gmlp_bs[l],
            'ssm_a_re': ssm_a_re[l], 'ssm_a_im': ssm_a_im[l], 'ssm_log_dt': ssm_log_dt[l],
            'ssm_b_re': ssm_b_re[l], 'ssm_b_im': ssm_b_im[l],
            'ssm_c_re': ssm_c_re[l], 'ssm_c_im': ssm_c_im[l], 'ssm_d': ssm_d[l],
            'ssm_glu_w': ssm_glu_w[l], 'ssm_glu_b': ssm_glu_b[l],
            'w_branch': w_branch[l], 'w_out': w_out[l],
            'ln1_g': ln1_g[l], 'ln1_b': ln1_b[l], 'ln2_g': ln2_g[l], 'ln2_b': ln2_b[l],
            'w_ffn_in': w_ffn_in[l], 'w_ffn_out': w_ffn_out[l],
        }
        yp, (kp, vp, _, hrp, hip) = trunk_layer(yp, pos_p, l, P, None)
        kp_l.append(kp.reshape(bp, Lp // PAGE_SIZE, PAGE_SIZE, N_HEADS_A, 2 * HEAD_DIM))
        vp_l.append(vp.reshape(bp, Lp // PAGE_SIZE, PAGE_SIZE, N_HEADS_A, 2 * HEAD_DIM))
        hrp_l.append(hrp)
        hip_l.append(hip)
        k_past = cache_k[l][page_table].reshape(bs_, n_past, N_HEADS_A, 2, HEAD_DIM)
        v_past = cache_v[l][page_table].reshape(bs_, n_past, N_HEADS_A, 2 * HEAD_DIM)
        past = (k_past, v_past, state_ssm_re[l], state_ssm_im[l])
        ys, (ks, vs, gvs, hrs, his) = trunk_layer(ys, pos_s, l, P, past)
        ks_l.append(ks.reshape(bs_, Ls, N_HEADS_A, 2 * HEAD_DIM))
        vs_l.append(vs)
        hrs_l.append(hrs)
        his_l.append(his)
        gvs_l.append(gvs)
    return (yp, ys,
            jnp.stack(kp_l), jnp.stack(vp_l), jnp.stack(ks_l), jnp.stack(vs_l),
            jnp.stack(hrp_l), jnp.stack(hip_l), jnp.stack(hrs_l), jnp.stack(his_l),
            jnp.stack(gvs_l))
```

```python
import functools
import math

import jax
import jax.numpy as jnp
from jax import lax
from jax.experimental import pallas as pl
from jax.experimental.pallas import tpu as pltpu

BF = jnp.bfloat16
F32 = jnp.float32

LN_EPS = 1e-5
ROPE_THETA = 10000.0
NEG_BIG = -1e30
LANES = 128
SUBLANES = 8
MXU_K = 256
VMEM_LIMIT = 56 * 1024 * 1024


def _params(*sem):
    return pltpu.CompilerParams(dimension_semantics=sem, vmem_limit_bytes=VMEM_LIMIT)


def _layer_norm(r, g, b):
    mu = jnp.mean(r, axis=-1, keepdims=True)
    d = r - mu
    var = jnp.mean(d * d, axis=-1, keepdims=True)
    return d * lax.rsqrt(var + LN_EPS) * g + b


def _in_proj_kernel(x_ref, w_ref, cos_ref, sin_ref,
                    q_ref, kf_ref, kb_ref, vf_ref, vb_ref, gu_ref, gv_ref, su_ref, gates_ref,
                    xb_ref, *, hd, q_scale):
    j = pl.program_id(1)

    @pl.when(j == 0)
    def _():
        xb_ref[...] = x_ref[...].astype(BF)

    acc = jnp.dot(xb_ref[...], w_ref[...], preferred_element_type=F32)
    half = hd // 2

    def rope(a):
        cos = cos_ref[...]
        sin = sin_ref[...]
        outs = []
        for c in range(a.shape[1] // LANES):
            blk = a[:, c * LANES:(c + 1) * LANES]
            lane = lax.broadcasted_iota(jnp.int32, blk.shape, 1)
            swapped = jnp.where((lane % hd) < half,
                                pltpu.roll(blk, LANES - half, 1), pltpu.roll(blk, half, 1))
            outs.append(blk * cos + swapped * sin)
        return jnp.concatenate(outs, axis=1)

    @pl.when(j == 0)
    def _():
        q_ref[...] = (rope(acc) * q_scale).astype(BF)

    @pl.when(j == 1)
    def _():
        k = rope(acc)
        kf_ref[...] = k
        kb_ref[...] = k.astype(BF)

    @pl.when(j == 2)
    def _():
        vf_ref[...] = acc
        vb_ref[...] = acc.astype(BF)

    @pl.when(j == 3)
    def _():
        gu_ref[...] = acc

    @pl.when(j == 4)
    def _():
        gv_ref[...] = acc

    @pl.when(j == 5)
    def _():
        su_ref[...] = acc

    @pl.when(j >= 6)
    def _():
        gates_ref[...] = acc


def _in_proj(x, w_b, cos_t, sin_t, *, wb, hd, tm, rows_per_seq, n_seq_cols):
    R, D = x.shape
    ncol = w_b.shape[1] // wb
    n_gate = ncol - 6
    nt = rows_per_seq // tm
    grid = (R // tm, ncol)
    if n_seq_cols > 1:
        su_shape = (rows_per_seq, n_seq_cols * wb)
        su_map = lambda i, j: (i % nt, i // nt)
    else:
        su_shape = (R, wb)
        su_map = lambda i, j: (i, 0)
    row_map = lambda i, j: (i, 0)
    tab_map = lambda i, j: (i % nt, 0)
    out_shape = (
        jax.ShapeDtypeStruct((R, wb), BF), jax.ShapeDtypeStruct((R, wb), F32),
        jax.ShapeDtypeStruct((R, wb), BF), jax.ShapeDtypeStruct((R, wb), F32),
        jax.ShapeDtypeStruct((R, wb), BF), jax.ShapeDtypeStruct((R, wb), F32),
        jax.ShapeDtypeStruct((R, wb), F32), jax.ShapeDtypeStruct(su_shape, F32),
        jax.ShapeDtypeStruct((R, n_gate * wb), F32))
    blk = lambda m: pl.BlockSpec((tm, wb), m)
    out_specs = (blk(row_map),) * 7 + (blk(su_map), blk(lambda i, j: (i, jnp.maximum(j - 6, 0))))
    return pl.pallas_call(
        functools.partial(_in_proj_kernel, hd=hd, q_scale=hd ** -0.5),
        grid=grid,
        in_specs=[pl.BlockSpec((tm, D), row_map),
                  pl.BlockSpec((D, wb), lambda i, j: (0, j)),
                  pl.BlockSpec((tm, LANES), tab_map),
                  pl.BlockSpec((tm, LANES), tab_map)],
        out_specs=out_specs,
        out_shape=out_shape,
        scratch_shapes=[pltpu.VMEM((tm, D), BF)],
        compiler_params=_params("parallel", "arbitrary"),
        name="in_proj",
    )(x, w_b, cos_t, sin_t)


def _softmax_step(s, v, m_ref, l_ref, acc_ref):
    m_prev = m_ref[...]
    m_new = jnp.maximum(m_prev, jnp.max(s, axis=1, keepdims=True))
    alpha = jnp.exp(m_prev - m_new)
    p = jnp.exp(s - m_new[:, :1])
    l_ref[...] = alpha * l_ref[...] + jnp.sum(p, axis=1, keepdims=True)
    acc_ref[...] = alpha * acc_ref[...] + jnp.dot(p.astype(BF), v, preferred_element_type=F32)
    m_ref[...] = m_new


def _attn_prompt_kernel(lam_ref, q_ref, k_ref, v_ref, g_ref, o_ref,
                        qs_ref, m_ref, l_ref, acc_ref, *, tq, tk, hd, post_scale):
    qi = pl.program_id(2)
    q = q_ref[...].astype(F32)
    lane = lax.broadcasted_iota(jnp.int32, q.shape, 1)
    qs_ref[0:tq, :] = jnp.where(lane < hd, q, 0.0).astype(BF)
    qs_ref[tq:2 * tq, :] = jnp.where(lane >= hd, q, 0.0).astype(BF)
    m_ref[...] = jnp.full(m_ref.shape, -jnp.inf, F32)
    l_ref[...] = jnp.zeros(l_ref.shape, F32)
    acc_ref[...] = jnp.zeros(acc_ref.shape, F32)

    def scores(ki):
        k0 = pl.multiple_of(ki * tk, tk)
        k = k_ref[pl.ds(k0, tk), :]
        v = v_ref[pl.ds(k0, tk), :]
        s = lax.dot_general(qs_ref[...], k, (((1,), (1,)), ((), ())), preferred_element_type=F32)
        return s, v

    n_full = (qi * tq) // tk

    def full_tile(ki, carry):
        s, v = scores(ki)
        _softmax_step(s, v, m_ref, l_ref, acc_ref)
        return carry

    lax.fori_loop(0, n_full, full_tile, 0)

    s, v = scores(n_full)
    row = lax.broadcasted_iota(jnp.int32, s.shape, 0)
    col = lax.broadcasted_iota(jnp.int32, s.shape, 1)
    qpos = qi * tq + row % tq
    kpos = n_full * tk + col
    _softmax_step(jnp.where(kpos <= qpos, s, NEG_BIG), v, m_ref, l_ref, acc_ref)

    t = acc_ref[...] / l_ref[...]
    o = t[0:tq] - lam_ref[0] * t[tq:2 * tq]
    ms = jnp.mean(o * o, axis=1, keepdims=True)
    o_ref[...] = (o * lax.rsqrt(ms + LN_EPS) * g_ref[...] * post_scale).astype(BF)


def _attn_prompt(lam, q, kb, vb, g, *, n_seq, seq, n_heads, hd, post_scale):
    tq = min(256, seq)
    tk = min(512, seq)
    dv = 2 * hd
    nq = seq // tq
    return pl.pallas_call(
        functools.partial(_attn_prompt_kernel, tq=tq, tk=tk, hd=hd, post_scale=post_scale),
        grid=(n_seq, n_heads, nq),
        in_specs=[pl.BlockSpec(memory_space=pltpu.SMEM),
                  pl.BlockSpec((tq, dv), lambda b, h, i: (b * nq + i, h)),
                  pl.BlockSpec((seq, dv), lambda b, h, i: (b, h)),
                  pl.BlockSpec((seq, dv), lambda b, h, i: (b, h)),
                  pl.BlockSpec((1, dv), lambda b, h, i: (0, 0))],
        out_specs=pl.BlockSpec((tq, dv), lambda b, h, i: (b * nq + i, h)),
        out_shape=jax.ShapeDtypeStruct(q.shape, BF),
        scratch_shapes=[pltpu.VMEM((2 * tq, dv), BF), pltpu.VMEM((2 * tq, dv), F32),
                        pltpu.VMEM((2 * tq, dv), F32), pltpu.VMEM((2 * tq, dv), F32)],
        compiler_params=_params("parallel", "parallel", "arbitrary"),
        name="attn_prompt",
    )(lam, q, kb, vb, g)


def _attn_sample_kernel(pt_ref, lam_ref, q_ref, kn_ref, vn_ref, g_ref, *rest,
                        n_pg, n_heads, hd, nq, post_scale):
    k_refs = rest[:n_pg]
    v_refs = rest[n_pg:2 * n_pg]
    o_ref, qs_ref, m_ref, l_ref, acc_ref = rest[2 * n_pg:]
    j = pl.program_id(1)
    dv = 2 * hd
    nr = qs_ref.shape[1]
    nt_dims = (((1,), (1,)), ((), ()))

    @pl.when(j == 0)
    def _():
        for h in range(n_heads):
            qh = q_ref[:, h * dv:(h + 1) * dv].astype(F32)
            row = lax.broadcasted_iota(jnp.int32, qh.shape, 0)
            lane = lax.broadcasted_iota(jnp.int32, qh.shape, 1)
            keep = (row < nr // 2) == (lane < hd)
            qs_ref[h] = jnp.where(keep, qh, 0.0).astype(BF)
        m_ref[...] = jnp.full(m_ref.shape, -jnp.inf, F32)
        l_ref[...] = jnp.zeros(l_ref.shape, F32)
        acc_ref[...] = jnp.zeros(acc_ref.shape, F32)

    def update(h, s, pv_fn):
        m_prev = m_ref[h]
        m_new = jnp.maximum(m_prev, jnp.max(s, axis=1, keepdims=True))
        alpha = jnp.exp(m_prev - m_new)
        p = jnp.exp(s - m_new[:, :1])
        l_ref[h] = alpha * l_ref[h] + jnp.sum(p, axis=1, keepdims=True)
        acc_ref[h] = alpha * acc_ref[h] + pv_fn(p.astype(BF))
        m_ref[h] = m_new

    for h in range(n_heads):
        hs = slice(h * dv, (h + 1) * dv)
        qs = qs_ref[h]
        s = jnp.concatenate(
            [lax.dot_general(qs, k_refs[g][:, hs].astype(BF), nt_dims, preferred_element_type=F32)
             for g in range(n_pg)], axis=1)
        page = k_refs[0].shape[0]

        def pv_past(p, hs=hs, page=page):
            out = jnp.dot(p[:, 0:page], v_refs[0][:, hs].astype(BF), preferred_element_type=F32)
            for g in range(1, n_pg):
                out += jnp.dot(p[:, g * page:(g + 1) * page], v_refs[g][:, hs].astype(BF),
                               preferred_element_type=F32)
            return out

        update(h, s, pv_past)

    @pl.when(j == pl.num_programs(1) - 1)
    def _():
        for h in range(n_heads):
            hs = slice(h * dv, (h + 1) * dv)
            s = lax.dot_general(qs_ref[h], kn_ref[:, hs], nt_dims, preferred_element_type=F32)
            row = lax.broadcasted_iota(jnp.int32, s.shape, 0)
            col = lax.broadcasted_iota(jnp.int32, s.shape, 1)
            s = jnp.where(col <= row % (nr // 2), s, NEG_BIG)
            update(h, s, lambda p, hs=hs: jnp.dot(p, vn_ref[:, hs], preferred_element_type=F32))
            t = acc_ref[h] / l_ref[h]
            o = t[0:nr // 2] - lam_ref[0] * t[nr // 2:nr]
            ms = jnp.mean(o * o, axis=1, keepdims=True)
            o_ref[:, hs] = o * lax.rsqrt(ms + LN_EPS) * g_ref[...] * post_scale


def _attn_sample(page_table_flat, lam, q16, kn, vn, g, cache_k, cache_v, *,
                 layer, n_seq, n_pages, n_heads, hd, nq, post_scale):
    n_pg = math.gcd(16, n_pages)
    nr = 16
    page, w = cache_k.shape[2], cache_k.shape[3]
    dv = 2 * hd

    def kv_spec(g):
        return pl.BlockSpec((None, None, page, w),
                            lambda b, j, pt: (layer, pt[b * n_pages + j * n_pg + g], 0, 0))

    grid_spec = pltpu.PrefetchScalarGridSpec(
        num_scalar_prefetch=1,
        grid=(n_seq, n_pages // n_pg),
        in_specs=[pl.BlockSpec(memory_space=pltpu.SMEM),
                  pl.BlockSpec((nr, w), lambda b, j, pt: (b, 0)),
                  pl.BlockSpec((page, w), lambda b, j, pt: (b, 0)),
                  pl.BlockSpec((page, w), lambda b, j, pt: (b, 0)),
                  pl.BlockSpec((1, dv), lambda b, j, pt: (0, 0))]
                 + [kv_spec(g) for g in range(n_pg)] + [kv_spec(g) for g in range(n_pg)],
        out_specs=pl.BlockSpec((nr // 2, w), lambda b, j, pt: (b, 0)),
        scratch_shapes=[pltpu.VMEM((n_heads, nr, dv), BF), pltpu.VMEM((n_heads, nr, dv), F32),
                        pltpu.VMEM((n_heads, nr, dv), F32), pltpu.VMEM((n_heads, nr, dv), F32)])
    return pl.pallas_call(
        functools.partial(_attn_sample_kernel, n_pg=n_pg, n_heads=n_heads, hd=hd, nq=nq,
                          post_scale=post_scale),
        grid_spec=grid_spec,
        out_shape=jax.ShapeDtypeStruct((n_seq * (nr // 2), w), F32),
        compiler_params=_params("parallel", "arbitrary"),
        name="attn_sample",
    )(page_table_flat, lam, q16, kn, vn, g, *([cache_k] * n_pg), *([cache_v] * n_pg))


def _gmlp_kernel(gu_ref, gv_ref, lng_ref, lnb_ref, ws_ref, bias_ref, out_ref, *vn_out,
                 n_chunk, n_grp, chunk, gc):
    vn = _layer_norm(jax.nn.gelu(gv_ref[...]), lng_ref[...], lnb_ref[...])
    if vn_out:
        vn_out[0][...] = vn
    vb = vn.astype(BF)
    u = jax.nn.gelu(gu_ref[...])
    for n in range(n_chunk):
        rs = slice(n * chunk, (n + 1) * chunk)
        for g in range(n_grp):
            cs = slice(g * gc, (g + 1) * gc)
            mixed = jnp.dot(ws_ref[g], vb[rs, cs], preferred_element_type=F32) + bias_ref[:, cs]
            out_ref[rs, cs] = (u[rs, cs] * mixed).astype(BF)


def _gmlp(gu, gv, ln_g, ln_b, ws_b, bias_full, *, tm, want_vn):
    R, w = gu.shape
    n_grp, chunk = ws_b.shape[0], ws_b.shape[1]
    row_map = lambda i: (i, 0)
    const2 = lambda i: (0, 0)
    out_shape = [jax.ShapeDtypeStruct((R, w), BF)]
    out_specs = [pl.BlockSpec((tm, w), row_map)]
    if want_vn:
        out_shape.append(jax.ShapeDtypeStruct((R, w), F32))
        out_specs.append(pl.BlockSpec((tm, w), row_map))
    res = pl.pallas_call(
        functools.partial(_gmlp_kernel, n_chunk=tm // chunk, n_grp=n_grp, chunk=chunk, gc=w // n_grp),
        grid=(R // tm,),
        in_specs=[pl.BlockSpec((tm, w), row_map), pl.BlockSpec((tm, w), row_map),
                  pl.BlockSpec((1, w), const2), pl.BlockSpec((1, w), const2),
                  pl.BlockSpec(ws_b.shape, lambda i: (0, 0, 0)),
                  pl.BlockSpec((chunk, w), const2)],
        out_specs=tuple(out_specs),
        out_shape=tuple(out_shape),
        compiler_params=_params("parallel"),
        name="gmlp",
    )(gu, gv, ln_g, ln_b, ws_b, bias_full)
    return res if want_vn else (res[0], None)


def _ssm_input_proj(u, bw_ref, xre_ref, xim_ref):
    ub = u.astype(BF)
    n_slab = bw_ref.shape[0]
    ns = xre_ref.shape[1] // n_slab
    for h in range(n_slab):
        uh = ub[:, h * MXU_K:(h + 1) * MXU_K]
        xre_ref[:, h * ns:(h + 1) * ns] = jnp.dot(uh, bw_ref[h, :, 0:ns], preferred_element_type=F32)
        xim_ref[:, h * ns:(h + 1) * ns] = jnp.dot(uh, bw_ref[h, :, ns:2 * ns], preferred_element_type=F32)


def _ssm_output(u, hre_ref, him_ref, cw_ref, d_ref, gw_ref, gb_ref, out_ref):
    n_slab = cw_ref.shape[1]
    ns = hre_ref.shape[1] // n_slab
    ys = []
    for h in range(n_slab):
        ss = slice(h * ns, (h + 1) * ns)
        ys.append(jnp.dot(hre_ref[:, ss].astype(BF), cw_ref[0, h], preferred_element_type=F32)
                  + jnp.dot(him_ref[:, ss].astype(BF), cw_ref[1, h], preferred_element_type=F32))
    y = jnp.concatenate(ys, axis=1) + d_ref[...] * u
    a = jax.nn.gelu(y)
    z = jnp.dot(a.astype(BF), gw_ref[...], preferred_element_type=F32) + gb_ref[...]
    out_ref[...] = (a * jax.nn.sigmoid(z)).astype(BF)


def _ssm_prompt_kernel(u_ref, bw_ref, cw_ref, pq_ref, d_ref, gw_ref, gb_ref,
                       out_ref, hre_out, him_out, xre_ref, xim_ref, cre_ref, cim_ref, *, lane_grp):
    @pl.when(pl.program_id(0) == 0)
    def _():
        cre_ref[...] = jnp.zeros(cre_ref.shape, F32)
        cim_ref[...] = jnp.zeros(cim_ref.shape, F32)

    u = u_ref[...]
    _ssm_input_proj(u, bw_ref, xre_ref, xim_ref)

    rows, ns = xre_ref.shape
    half = SUBLANES // 2
    for lg in range(ns // lane_grp):
        sl = slice(lg * lane_grp, (lg + 1) * lane_grp)
        p_re, p_im, q_re, q_im = pq_ref[0, :, sl], pq_ref[1, :, sl], pq_ref[2, :, sl], pq_ref[3, :, sl]
        upper = lax.broadcasted_iota(jnp.int32, (SUBLANES, lane_grp), 0) >= half

        def body(i, carry, sl=sl, p_re=p_re, p_im=p_im, q_re=q_re, q_im=q_im, upper=upper):
            c_re, c_im = carry
            r0 = pl.multiple_of(i * SUBLANES, SUBLANES)
            x_re = xre_ref[pl.ds(r0, SUBLANES), sl]
            x_im = xim_ref[pl.ds(r0, SUBLANES), sl]
            s_re = pltpu.roll(x_re, half, 0)
            s_im = pltpu.roll(x_im, half, 0)
            h_re = x_re + q_re * s_re - q_im * s_im + p_re * c_re - p_im * c_im
            h_im = x_im + q_re * s_im + q_im * s_re + p_re * c_im + p_im * c_re
            xre_ref[pl.ds(r0, SUBLANES), sl] = h_re
            xim_ref[pl.ds(r0, SUBLANES), sl] = h_im
            return (jnp.where(upper, h_re, pltpu.roll(h_re, half, 0)),
                    jnp.where(upper, h_im, pltpu.roll(h_im, half, 0)))

        c_re, c_im = lax.fori_loop(0, rows // SUBLANES, body, (cre_ref[:, sl], cim_ref[:, sl]), unroll=2)
        cre_ref[:, sl] = c_re
        cim_ref[:, sl] = c_im

    _ssm_output(u, xre_ref, xim_ref, cw_ref, d_ref, gw_ref, gb_ref, out_ref)
    hre_out[...] = cre_ref[...]
    him_out[...] = cim_ref[...]


def _ssm_prompt(su_tb, bw, cw, pq, d, gw, gb, *, n_seq, t_chunk):
    rows_total, w = su_tb.shape
    ns = pq.shape[2]
    rows = n_seq * t_chunk
    row_map = lambda i: (i, 0)
    const2 = lambda i: (0, 0)
    return pl.pallas_call(
        functools.partial(_ssm_prompt_kernel, lane_grp=min(512, ns)),
        grid=(rows_total // rows,),
        in_specs=[pl.BlockSpec((rows, w), row_map),
                  pl.BlockSpec(bw.shape, lambda i: (0, 0, 0)),
                  pl.BlockSpec(cw.shape, lambda i: (0, 0, 0, 0)),
                  pl.BlockSpec(pq.shape, lambda i: (0, 0, 0)),
                  pl.BlockSpec((1, w), const2),
                  pl.BlockSpec(gw.shape, const2),
                  pl.BlockSpec((1, w), const2)],
        out_specs=(pl.BlockSpec((rows, w), row_map),
                   pl.BlockSpec((SUBLANES, ns), const2), pl.BlockSpec((SUBLANES, ns), const2)),
        out_shape=(jax.ShapeDtypeStruct((rows_total, w), BF),
                   jax.ShapeDtypeStruct((SUBLANES, ns), F32), jax.ShapeDtypeStruct((SUBLANES, ns), F32)),
        scratch_shapes=[pltpu.VMEM((rows, ns), F32), pltpu.VMEM((rows, ns), F32),
                        pltpu.VMEM((SUBLANES, ns), F32), pltpu.VMEM((SUBLANES, ns), F32)],
        compiler_params=_params("arbitrary"),
        name="ssm_prompt",
    )(su_tb, bw, cw, pq, d, gw, gb)


def _ssm_sample_kernel(u_ref, h0re_ref, h0im_ref, bw_ref, cw_ref, a_ref, d_ref, gw_ref, gb_ref,
                       out_ref, hre_out, him_out, xre_ref, xim_ref, *, n_steps):
    u = u_ref[...]
    _ssm_input_proj(u, bw_ref, xre_ref, xim_ref)
    nb = h0re_ref.shape[0]
    a_re, a_im = a_ref[0], a_ref[1]
    h_re, h_im = h0re_ref[...], h0im_ref[...]
    for t in range(n_steps):
        rs = slice(t * nb, (t + 1) * nb)
        h_re, h_im = (a_re * h_re - a_im * h_im + xre_ref[rs, :],
                      a_re * h_im + a_im * h_re + xim_ref[rs, :])
        xre_ref[rs, :] = h_re
        xim_ref[rs, :] = h_im
    _ssm_output(u, xre_ref, xim_ref, cw_ref, d_ref, gw_ref, gb_ref, out_ref)
    hre_out[...] = h_re
    him_out[...] = h_im


def _ssm_sample(su, h0_re, h0_im, bw, cw, a, d, gw, gb, *, n_steps):
    rows, w = su.shape
    nb, ns = h0_re.shape
    full = lambda arr: pl.BlockSpec(arr.shape, lambda i, n=arr.ndim: (0,) * n)
    return pl.pallas_call(
        functools.partial(_ssm_sample_kernel, n_steps=n_steps),
        grid=(1,),
        in_specs=[full(su), full(h0_re), full(h0_im), full(bw), full(cw), full(a), full(d), full(gw), full(gb)],
        out_specs=(pl.BlockSpec((rows, w), lambda i: (0, 0)),
                   pl.BlockSpec((nb, ns), lambda i: (0, 0)), pl.BlockSpec((nb, ns), lambda i: (0, 0))),
        out_shape=(jax.ShapeDtypeStruct((rows, w), BF),
                   jax.ShapeDtypeStruct((nb, ns), F32), jax.ShapeDtypeStruct((nb, ns), F32)),
        scratch_shapes=[pltpu.VMEM((rows, ns), F32), pltpu.VMEM((rows, ns), F32)],
        compiler_params=_params("arbitrary"),
        name="ssm_sample",
    )(su, h0_re, h0_im, bw, cw, a, d, gw, gb)


def _merge_kernel(at_ref, gm_ref, ss_ref, gates_ref, bg_ref, x_ref, wb_ref, wo_ref, lg_ref, lb_ref,
                  o_ref, *, alpha):
    d = x_ref.shape[1]
    merged = None
    for n, br in enumerate((at_ref, gm_ref, ss_ref)):
        proj = jnp.dot(br[...], wb_ref[n], preferred_element_type=F32)
        gate = jax.nn.sigmoid(gates_ref[:, n * d:(n + 1) * d] + bg_ref[:, n * d:(n + 1) * d])
        merged = gate * proj if merged is None else merged + gate * proj
    mix = jnp.dot(merged.astype(BF), wo_ref[...], preferred_element_type=F32)
    o_ref[...] = _layer_norm(alpha * x_ref[...] + mix, lg_ref[...], lb_ref[...])


def _merge(attn, gm, ssm, gates, b_gate, x, wb_b, wo_b, ln_g, ln_b, *, tm, alpha, rows_per_seq, ssm_seq_cols):
    R, d = x.shape
    w = attn.shape[1]
    nt = rows_per_seq // tm
    row_map = lambda i: (i, 0)
    const2 = lambda i: (0, 0)
    ssm_map = (lambda i: (i % nt, i // nt)) if ssm_seq_cols > 1 else row_map
    return pl.pallas_call(
        functools.partial(_merge_kernel, alpha=alpha),
        grid=(R // tm,),
        in_specs=[pl.BlockSpec((tm, w), row_map), pl.BlockSpec((tm, w), row_map),
                  pl.BlockSpec((tm, w), ssm_map),
                  pl.BlockSpec((tm, gates.shape[1]), row_map),
                  pl.BlockSpec((1, gates.shape[1]), const2),
                  pl.BlockSpec((tm, d), row_map),
                  pl.BlockSpec(wb_b.shape, lambda i: (0, 0, 0)),
                  pl.BlockSpec(wo_b.shape, const2),
                  pl.BlockSpec((1, d), const2), pl.BlockSpec((1, d), const2)],
        out_specs=pl.BlockSpec((tm, d), row_map),
        out_shape=jax.ShapeDtypeStruct((R, d), F32),
        compiler_params=_params("parallel"),
        name="merge",
    )(attn, gm, ssm, gates, b_gate, x, wb_b, wo_b, ln_g, ln_b)


def _ffn_kernel(x_ref, wg_ref, wu_ref, wo_ref, lg_ref, lb_ref, o_ref, xb_ref, acc_ref, *, alpha):
    j = pl.program_id(1)

    @pl.when(j == 0)
    def _():
        xb_ref[...] = x_ref[...].astype(BF)
        acc_ref[...] = jnp.zeros(acc_ref.shape, F32)

    xb = xb_ref[...]
    gate = jnp.dot(xb, wg_ref[...], preferred_element_type=F32)
    up = jnp.dot(xb, wu_ref[...], preferred_element_type=F32)
    hid = (gate * jax.nn.sigmoid(gate) * up).astype(BF)
    acc_ref[...] += jnp.dot(hid, wo_ref[...], preferred_element_type=F32)

    @pl.when(j == pl.num_programs(1) - 1)
    def _():
        o_ref[...] = _layer_norm(alpha * x_ref[...] + acc_ref[...], lg_ref[...], lb_ref[...])


def _ffn(x, w_in_b, w_out_b, ln_g, ln_b, *, tm, tf, alpha):
    R, d = x.shape
    d_ff = w_out_b.shape[0]
    nf = d_ff // tf
    row_map = lambda i, j: (i, 0)
    const2 = lambda i, j: (0, 0)
    return pl.pallas_call(
        functools.partial(_ffn_kernel, alpha=alpha),
        grid=(R // tm, nf),
        in_specs=[pl.BlockSpec((tm, d), row_map),
                  pl.BlockSpec((d, tf), lambda i, j: (0, j)),
                  pl.BlockSpec((d, tf), lambda i, j: (0, j + nf)),
                  pl.BlockSpec((tf, d), lambda i, j: (j, 0)),
                  pl.BlockSpec((1, d), const2), pl.BlockSpec((1, d), const2)],
        out_specs=pl.BlockSpec((tm, d), row_map),
        out_shape=jax.ShapeDtypeStruct((R, d), F32),
        scratch_shapes=[pltpu.VMEM((tm, d), BF), pltpu.VMEM((tm, d), F32)],
        compiler_params=_params("parallel", "arbitrary"),
        name="ffn",
    )(x, w_in_b, w_in_b, w_out_b, ln_g, ln_b)


def _rope_tables(pos, hd):
    half = hd // 2
    inv = ROPE_THETA ** (-jnp.arange(half, dtype=F32) / half)
    ang = pos.astype(F32)[:, None] * inv[None, :]
    cos, sin = jnp.cos(ang), jnp.sin(ang)
    reps = LANES // hd
    return (jnp.tile(jnp.concatenate([cos, cos], axis=1), (1, reps)),
            jnp.tile(jnp.concatenate([-sin, sin], axis=1), (1, reps)))


def _ssm_discretize(a_re, a_im, log_dt, b_re, b_im):
    dt = jnp.exp(log_dt.astype(F32))[:, None]
    ar, ai = a_re.astype(F32), a_im.astype(F32)
    mag = jnp.exp(dt * ar)
    abar_re = mag * jnp.cos(dt * ai)
    abar_im = mag * jnp.sin(dt * ai)
    nr, ni = abar_re - 1.0, abar_im
    den = ar * ar + ai * ai
    c_re = (nr * ar + ni * ai) / den
    c_im = (ni * ar - nr * ai) / den
    br, bi = b_re.astype(F32), b_im.astype(F32)
    bbar_re = c_re[..., None] * br - c_im[..., None] * bi
    bbar_im = c_re[..., None] * bi + c_im[..., None] * br
    return abar_re, abar_im, bbar_re, bbar_im


def _ssm_weights(a_re, a_im, log_dt, b_re, b_im, c_re, c_im):
    n_grp, n_state, grp_ch = b_re.shape
    abar_re, abar_im, bbar_re, bbar_im = _ssm_discretize(a_re, a_im, log_dt, b_re, b_im)
    gps = MXU_K // grp_ch
    n_slab = n_grp // gps
    eye = jnp.eye(gps, dtype=F32)

    def in_blocks(bb):
        bb = bb.reshape(n_slab, gps, n_state, grp_ch)
        return jnp.einsum('sgpc,gk->sgckp', bb, eye).reshape(n_slab, gps * grp_ch, gps * n_state)

    def out_blocks(cc):
        cc = cc.reshape(n_slab, gps, grp_ch, n_state)
        return jnp.einsum('sgcp,gk->sgpkc', cc, eye).reshape(n_slab, gps * n_state, gps * grp_ch)

    bw = jnp.concatenate([in_blocks(bbar_re), in_blocks(bbar_im)], axis=2).astype(BF)
    cw = jnp.stack([out_blocks(c_re.astype(F32)), -out_blocks(c_im.astype(F32))]).astype(BF)
    a_flat = jnp.stack([abar_re.reshape(1, -1), abar_im.reshape(1, -1)])
    a2_re = abar_re * abar_re - abar_im * abar_im
    a2_im = 2.0 * abar_re * abar_im
    half = SUBLANES // 2

    def halves(lo, hi):
        return jnp.concatenate([jnp.broadcast_to(lo.reshape(1, -1), (half, lo.size)),
                                jnp.broadcast_to(hi.reshape(1, -1), (half, hi.size))], axis=0)

    zero = jnp.zeros_like(abar_re)
    pq = jnp.stack([halves(abar_re, a2_re), halves(abar_im, a2_im),
                    halves(zero, abar_re), halves(zero, abar_im)])
    return bw, cw, a_flat, pq


def _lambda_init(l):
    return 0.8 - 0.6 * math.exp(-0.3 * l)


def kernel(x_prompt, x_sample, cache_k, cache_v, page_table, state_ssm_re, state_ssm_im, w_in, b_gate, lambda_q1, lambda_k1, lambda_q2, lambda_k2, attn_norm_g, gmlp_ln_g, gmlp_ln_b, gmlp_ws, gmlp_bs, ssm_a_re, ssm_a_im, ssm_log_dt, ssm_b_re, ssm_b_im, ssm_c_re, ssm_c_im, ssm_d, ssm_glu_w, ssm_glu_b, w_branch, w_out, ln1_g, ln1_b, ln2_g, ln2_b, w_ffn_in, w_ffn_out):
    bp, lp, d_model = x_prompt.shape
    bs, ls, _ = x_sample.shape
    depth = w_in.shape[0]
    n_pool, page, n_heads, dv = cache_k.shape[1:]
    hd = dv // 2
    n_pages = page_table.shape[1]
    n_past = n_pages * page
    wbr = ssm_glu_w.shape[1]
    n_grp_g, chunk = gmlp_ws.shape[1], gmlp_ws.shape[2]
    n_grp_s, n_state, grp_ch = ssm_b_re.shape[1:]
    ns = n_grp_s * n_state
    d_ff = w_ffn_out.shape[1]
    alpha = (2 * depth) ** 0.25
    assert bp == SUBLANES // 2 and n_heads * dv == wbr and wbr % MXU_K == 0 and LANES % hd == 0
    assert ls <= SUBLANES and ls <= chunk and (bs * ls) % chunk == 0 and lp % page == 0

    rp, rs = bp * lp, bs * ls
    xp = x_prompt.reshape(rp, d_model)
    xs = x_sample.transpose(1, 0, 2).reshape(rs, d_model)
    cos_p, sin_p = _rope_tables(jnp.arange(lp), hd)
    cos_s, sin_s = _rope_tables(jnp.repeat(n_past + jnp.arange(ls), bs), hd)
    ck = cache_k.reshape(depth, n_pool, page, wbr)
    cv = cache_v.reshape(depth, n_pool, page, wbr)
    pt_flat = page_table.reshape(-1).astype(jnp.int32)

    tm_p = min(512, lp)
    tf = 256 if d_ff % 256 == 0 else d_ff
    t_chunk = min(256, lp)

    outs = {k: [] for k in ("kp", "vp", "ks", "vs", "hrp", "hip", "hrs", "his", "gvs")}
    for l in range(depth):
        lam_init = _lambda_init(l)
        post_scale = 1.0 - lam_init
        lam = (jnp.exp(jnp.sum(lambda_q1[l].astype(F32) * lambda_k1[l].astype(F32)))
               - jnp.exp(jnp.sum(lambda_q2[l].astype(F32) * lambda_k2[l].astype(F32)))
               + lam_init).reshape(1)
        w_in_b = w_in[l].astype(BF)
        wb_b = w_branch[l].astype(BF)
        wo_b = w_out[l].astype(BF)
        wfi_b = w_ffn_in[l].astype(BF)
        wfo_b = w_ffn_out[l].astype(BF)
        glu_b16 = ssm_glu_w[l].astype(BF)
        an_g = attn_norm_g[l].reshape(1, dv)
        ws_tril = jnp.tril(gmlp_ws[l])
        bw, cw, a_flat, pq = _ssm_weights(ssm_a_re[l], ssm_a_im[l], ssm_log_dt[l], ssm_b_re[l], ssm_b_im[l],
                                          ssm_c_re[l], ssm_c_im[l])
        d_row = ssm_d[l].reshape(1, wbr)
        glu_bias = ssm_glu_b[l].reshape(1, wbr)
        ln_gg, ln_gb = gmlp_ln_g[l].reshape(1, wbr), gmlp_ln_b[l].reshape(1, wbr)
        bgate = b_gate[l].reshape(1, -1)
        l1g, l1b = ln1_g[l].reshape(1, d_model), ln1_b[l].reshape(1, d_model)
        l2g, l2b = ln2_g[l].reshape(1, d_model), ln2_b[l].reshape(1, d_model)
        gc = wbr // n_grp_g

        q, kf, kb, vf, vb, gu, gv, su, gates = _in_proj(
            xp, w_in_b, cos_p, sin_p, wb=wbr, hd=hd, tm=tm_p, rows_per_seq=lp, n_seq_cols=bp)
        attn = _attn_prompt(lam, q, kb, vb, an_g, n_seq=bp, seq=lp, n_heads=n_heads, hd=hd,
                            post_scale=post_scale)
        bias_p = jnp.repeat(gmlp_bs[l].T, gc, axis=1)
        gm, _ = _gmlp(gu, gv, ln_gg, ln_gb, ws_tril.astype(BF), bias_p, tm=tm_p, want_vn=False)
        ssm, hre, him = _ssm_prompt(su.reshape(lp * bp, wbr), bw, cw, pq, d_row, glu_b16, glu_bias,
                                    n_seq=bp, t_chunk=t_chunk)
        x1 = _merge(attn, gm, ssm.reshape(lp, bp * wbr), gates, bgate, xp, wb_b, wo_b, l1g, l1b,
                    tm=tm_p, alpha=alpha, rows_per_seq=lp, ssm_seq_cols=bp)
        xp = _ffn(x1, wfi_b, wfo_b, l2g, l2b, tm=min(1024, rp), tf=tf, alpha=alpha)
        outs["kp"].append(kf.reshape(bp, lp // page, page, n_heads, dv))
        outs["vp"].append(vf.reshape(bp, lp // page, page, n_heads, dv))
        outs["hrp"].append(hre[SUBLANES // 2:].reshape(bp, n_grp_s, n_state))
        outs["hip"].append(him[SUBLANES // 2:].reshape(bp, n_grp_s, n_state))

        q, kf, kb, vf, vb, gu, gv, su, gates = _in_proj(
            xs, w_in_b, cos_s, sin_s, wb=wbr, hd=hd, tm=rs, rows_per_seq=rs, n_seq_cols=1)
        q3 = q.reshape(ls, bs, wbr).transpose(1, 0, 2)
        zq = jnp.zeros((bs, SUBLANES - ls, wbr), BF)
        q16 = jnp.concatenate([q3, zq, q3, zq], axis=1).reshape(bs * 2 * SUBLANES, wbr)
        pad_new = lambda a: jnp.pad(a.reshape(ls, bs, wbr).transpose(1, 0, 2),
                                    ((0, 0), (0, page - ls), (0, 0))).reshape(bs * page, wbr)
        attn8 = _attn_sample(pt_flat, lam, q16, pad_new(kb), pad_new(vb), an_g, ck, cv,
                             layer=l, n_seq=bs, n_pages=n_pages, n_heads=n_heads, hd=hd, nq=ls,
                             post_scale=post_scale)
        attn = attn8.reshape(bs, SUBLANES, wbr)[:, :ls].transpose(1, 0, 2).reshape(rs, wbr).astype(BF)
        ws_s = jnp.einsum('gts,bc->gtbsc', ws_tril[:, :ls, :ls], jnp.eye(bs, dtype=F32)).reshape(n_grp_g, rs, rs)
        bias_s = jnp.repeat(jnp.repeat(gmlp_bs[l][:, :ls].T, bs, axis=0), gc, axis=1)
        gm, vn = _gmlp(gu, gv, ln_gg, ln_gb, ws_s.astype(BF), bias_s, tm=rs, want_vn=True)
        ssm, hre, him = _ssm_sample(su, state_ssm_re[l].reshape(bs, ns), state_ssm_im[l].reshape(bs, ns),
                                    bw, cw, a_flat, d_row, glu_b16, glu_bias, n_steps=ls)
        x1 = _merge(attn, gm, ssm, gates, bgate, xs, wb_b, wo_b, l1g, l1b,
                    tm=rs, alpha=alpha, rows_per_seq=rs, ssm_seq_cols=1)
        xs = _ffn(x1, wfi_b, wfo_b, l2g, l2b, tm=rs, tf=tf, alpha=alpha)
        to_seq_major = lambda a: a.reshape(ls, bs, -1).transpose(1, 0, 2)
        outs["ks"].append(to_seq_major(kf).reshape(bs, ls, n_heads, dv))
        outs["vs"].append(to_seq_major(vf).reshape(bs, ls, n_heads, dv))
        outs["hrs"].append(hre.reshape(bs, n_grp_s, n_state))
        outs["his"].append(him.reshape(bs, n_grp_s, n_state))
        outs["gvs"].append(to_seq_major(vn))

    st = lambda k: jnp.stack(outs[k])
    return (xp.reshape(bp, lp, d_model), xs.reshape(ls, bs, d_model).transpose(1, 0, 2),
            st("kp"), st("vp"), st("ks"), st("vs"),
            st("hrp"), st("hip"), st("hrs"), st("his"), st("gvs"))
```

```python
import functools
import math

import jax
import jax.numpy as jnp
from jax import lax
from jax.experimental import pallas as pl
from jax.experimental.pallas import tpu as pltpu

BF = jnp.bfloat16
F32 = jnp.float32

LN_EPS = 1e-5
ROPE_THETA = 10000.0
NEG_BIG = -1e30
LOG2E = math.log2(math.e)
LANES = 128
SUBLANES = 8
MXU_K = 256
VMEM_LIMIT = 56 * 1024 * 1024


def _params(*sem):
    return pltpu.CompilerParams(dimension_semantics=sem, vmem_limit_bytes=VMEM_LIMIT)


def _layer_norm(r, g, b):
    mu = jnp.mean(r, axis=-1, keepdims=True)
    d = r - mu
    var = jnp.mean(d * d, axis=-1, keepdims=True)
    return d * lax.rsqrt(var + LN_EPS) * g + b


def _in_proj_kernel(x_ref, w_ref, cos_ref, sin_ref, bg_ref, lng_ref, lnb_ref, ws_ref, bias_ref, *rest,
                    hd, n_heads, q_scale, prompt, tk):
    if prompt:
        q_ref, kt_ref, kf_ref, vx_ref, vf_ref, gm_ref, su_ref, gates_ref, xb_ref, mixed_ref = rest
    else:
        q_ref, kb_ref, kf_ref, vb_ref, vf_ref, gm_ref, vn_ref, su_ref, gates_ref, xb_ref, mixed_ref = rest
    j = pl.program_id(1)
    tm = x_ref.shape[0]
    dv = 2 * hd

    @pl.when(j == 0)
    def _():
        xb_ref[...] = x_ref[...].astype(BF)

    acc = jnp.dot(xb_ref[...], w_ref[...], preferred_element_type=F32)
    half = hd // 2

    def rope(a):
        cos = cos_ref[...]
        sin = sin_ref[...]
        outs = []
        for c in range(a.shape[1] // LANES):
            blk = a[:, c * LANES:(c + 1) * LANES]
            lane = lax.broadcasted_iota(jnp.int32, blk.shape, 1)
            swapped = jnp.where((lane % hd) < half,
                                pltpu.roll(blk, LANES - half, 1), pltpu.roll(blk, half, 1))
            outs.append(blk * cos + swapped * sin)
        return jnp.concatenate(outs, axis=1)

    def store_head_rows(dst_ref, a):
        for h in range(n_heads):
            dst_ref[pl.ds(h, tm, stride=n_heads), :] = a[:, h * dv:(h + 1) * dv]

    @pl.when(j == 0)
    def _():
        q_ref[...] = (rope(acc) * q_scale).astype(BF)

    @pl.when(j == 1)
    def _():
        k = rope(acc)
        if prompt:
            store_head_rows(kf_ref, k)
            for t in range(tm // tk):
                kt_ref[t] = k[t * tk:(t + 1) * tk, :].T.astype(BF)
        else:
            kf_ref[...] = k
            kb_ref[...] = k.astype(BF)

    @pl.when(j == 2)
    def _():
        if prompt:
            store_head_rows(vf_ref, acc)
            ones = jnp.ones((tm, dv), BF)
            for h in range(n_heads):
                vx_ref[:, 2 * h * dv:(2 * h + 1) * dv] = acc[:, h * dv:(h + 1) * dv].astype(BF)
                vx_ref[:, (2 * h + 1) * dv:(2 * h + 2) * dv] = ones
        else:
            vf_ref[...] = acc
            vb_ref[...] = acc.astype(BF)

    @pl.when(j == 3)
    def _():
        vn = _layer_norm(jax.nn.gelu(acc), lng_ref[...], lnb_ref[...])
        if not prompt:
            vn_ref[...] = vn
        vb = vn.astype(BF)
        n_grp, chunk = ws_ref.shape[0], ws_ref.shape[1]
        gc = vb.shape[1] // n_grp
        for n in range(tm // chunk):
            rs = slice(n * chunk, (n + 1) * chunk)
            for g in range(n_grp):
                cs = slice(g * gc, (g + 1) * gc)
                mixed_ref[rs, cs] = (jnp.dot(ws_ref[g], vb[rs, cs], preferred_element_type=F32)
                                     + bias_ref[:, cs])

    @pl.when(j == 4)
    def _():
        gm_ref[...] = (jax.nn.gelu(acc) * mixed_ref[...]).astype(BF)

    @pl.when(j == 5)
    def _():
        su_ref[...] = acc.astype(BF)

    @pl.when(j >= 6)
    def _():
        gates_ref[...] = jax.nn.sigmoid(acc + bg_ref[...]).astype(BF)


def _in_proj(x, w_b, cos_t, sin_t, b_gate, ln_g, ln_b, ws_b, bias_full, *,
             wb, hd, n_heads, tm, tk, rows_per_seq, prompt):
    R, D = x.shape
    ncol = w_b.shape[1] // wb
    n_gate = ncol - 6
    nt = rows_per_seq // tm
    dv = 2 * hd
    chunk = ws_b.shape[1]
    row_map = lambda i, j: (i, 0)
    const2 = lambda i, j: (0, 0)
    tab_map = lambda i, j: (i % nt, 0)
    gate_blk = lambda j: jnp.maximum(j - 6, 0)
    w_col = lambda j: jnp.where(j == 3, 4, jnp.where(j == 4, 3, j))
    blk = lambda: pl.BlockSpec((tm, wb), row_map)
    sds = jax.ShapeDtypeStruct
    if prompt:
        out_shape = (sds((R, wb), BF), sds((R // tk, wb, tk), BF), sds((R * n_heads, dv), F32),
                     sds((R, 2 * wb), BF), sds((R * n_heads, dv), F32),
                     sds((R, wb), BF), sds((R, wb), BF), sds((R, n_gate * wb), BF))
        out_specs = (blk(), pl.BlockSpec((tm // tk, wb, tk), lambda i, j: (i, 0, 0)),
                     pl.BlockSpec((tm * n_heads, dv), row_map),
                     pl.BlockSpec((tm, 2 * wb), row_map), pl.BlockSpec((tm * n_heads, dv), row_map),
                     blk(), blk(), pl.BlockSpec((tm, wb), lambda i, j: (i, gate_blk(j))))
    else:
        out_shape = (sds((R, wb), BF), sds((R, wb), BF), sds((R, wb), F32), sds((R, wb), BF), sds((R, wb), F32),
                     sds((R, wb), BF), sds((R, wb), F32), sds((R, wb), BF), sds((R, n_gate * wb), BF))
        out_specs = (blk(),) * 8 + (pl.BlockSpec((tm, wb), lambda i, j: (i, gate_blk(j))),)
    return pl.pallas_call(
        functools.partial(_in_proj_kernel, hd=hd, n_heads=n_heads, q_scale=hd ** -0.5 * LOG2E,
                          prompt=prompt, tk=tk),
        grid=(R // tm, ncol),
        in_specs=[pl.BlockSpec((tm, D), row_map),
                  pl.BlockSpec((D, wb), lambda i, j: (0, w_col(j))),
                  pl.BlockSpec((tm, LANES), tab_map),
                  pl.BlockSpec((tm, LANES), tab_map),
                  pl.BlockSpec((1, wb), lambda i, j: (0, gate_blk(j))),
                  pl.BlockSpec((1, wb), const2), pl.BlockSpec((1, wb), const2),
                  pl.BlockSpec(ws_b.shape, lambda i, j: (0, 0, 0)),
                  pl.BlockSpec((chunk, wb), const2)],
        out_specs=out_specs,
        out_shape=out_shape,
        scratch_shapes=[pltpu.VMEM((tm, D), BF), pltpu.VMEM((tm, wb), F32)],
        compiler_params=_params("parallel", "arbitrary"),
        name="in_proj_prompt" if prompt else "in_proj_sample",
    )(x, w_b, cos_t, sin_t, b_gate, ln_g, ln_b, ws_b, bias_full)


def _attn_prompt_kernel(lam_ref, q_ref, kt_ref, vx_ref, g_ref, o_ref, qs_ref, m_ref, acc_ref,
                        *, tq, tk, hd, n_heads, post_scale):
    qi = pl.program_id(1)
    dv = 2 * hd
    for h in range(n_heads):
        q = q_ref[:, h * dv:(h + 1) * dv].astype(F32)
        lane = lax.broadcasted_iota(jnp.int32, q.shape, 1)
        qs_ref[h, 0:tq, :] = jnp.where(lane < hd, q, 0.0).astype(BF)
        qs_ref[h, tq:2 * tq, :] = jnp.where(lane >= hd, q, 0.0).astype(BF)
    m_ref[...] = jnp.full(m_ref.shape, -jnp.inf, F32)
    acc_ref[...] = jnp.zeros(acc_ref.shape, F32)

    def tile(ki, masked):
        k0 = pl.multiple_of(ki * tk, tk)
        for h in range(n_heads):
            s = jnp.dot(qs_ref[h], kt_ref[ki, h * dv:(h + 1) * dv, :], preferred_element_type=F32)
            if masked:
                row = lax.broadcasted_iota(jnp.int32, s.shape, 0)
                col = lax.broadcasted_iota(jnp.int32, s.shape, 1)
                s = jnp.where(ki * tk + col <= qi * tq + row % tq, s, NEG_BIG)
            m_prev = m_ref[h]
            m_new = jnp.maximum(m_prev, jnp.max(s, axis=1, keepdims=True))
            alpha = jnp.exp2(m_prev - m_new)
            p = jnp.exp2(s - m_new[:, :1]).astype(BF)
            pv = jnp.dot(p, vx_ref[pl.ds(k0, tk), 2 * h * dv:(2 * h + 2) * dv], preferred_element_type=F32)
            acc_ref[h, :, 0:dv] = alpha * acc_ref[h, :, 0:dv] + pv[:, 0:dv]
            acc_ref[h, :, dv:2 * dv] = alpha * acc_ref[h, :, dv:2 * dv] + pv[:, dv:2 * dv]
            m_ref[h] = m_new

    n_full = (qi * tq) // tk

    def full_tile(ki, carry):
        tile(ki, False)
        return carry

    lax.fori_loop(0, n_full, full_tile, 0)
    tile(n_full, True)

    for h in range(n_heads):
        t = acc_ref[h, :, 0:dv] / acc_ref[h, :, dv:2 * dv]
        o = t[0:tq] - lam_ref[0] * t[tq:2 * tq]
        ms = jnp.mean(o * o, axis=1, keepdims=True)
        o_ref[:, h * dv:(h + 1) * dv] = (o * lax.rsqrt(ms + LN_EPS) * g_ref[...] * post_scale).astype(BF)


def _attn_prompt(lam, q, kt, vx, g, *, n_seq, seq, n_heads, hd, tk, post_scale):
    tq = min(256, seq)
    dv = 2 * hd
    w = n_heads * dv
    nq = seq // tq
    return pl.pallas_call(
        functools.partial(_attn_prompt_kernel, tq=tq, tk=tk, hd=hd, n_heads=n_heads, post_scale=post_scale),
        grid=(n_seq, nq),
        in_specs=[pl.BlockSpec(memory_space=pltpu.SMEM),
                  pl.BlockSpec((tq, w), lambda b, i: (b * nq + i, 0)),
                  pl.BlockSpec((seq // tk, w, tk), lambda b, i: (b, 0, 0)),
                  pl.BlockSpec((seq, 2 * w), lambda b, i: (b, 0)),
                  pl.BlockSpec((1, dv), lambda b, i: (0, 0))],
        out_specs=pl.BlockSpec((tq, w), lambda b, i: (b * nq + i, 0)),
        out_shape=jax.ShapeDtypeStruct(q.shape, BF),
        scratch_shapes=[pltpu.VMEM((n_heads, 2 * tq, dv), BF), pltpu.VMEM((n_heads, 2 * tq, dv), F32),
                        pltpu.VMEM((n_heads, 2 * tq, 2 * dv), F32)],
        compiler_params=_params("parallel", "arbitrary"),
        name="attn_prompt",
    )(lam, q, kt, vx, g)


def _attn_sample_kernel(pt_ref, lam_ref, q_ref, kn_ref, vn_ref, g_ref, *rest,
                        n_pg, n_heads, hd, post_scale):
    k_refs = rest[:n_pg]
    v_refs = rest[n_pg:2 * n_pg]
    o_ref, qs_ref, m_ref, l_ref, acc_ref = rest[2 * n_pg:]
    j = pl.program_id(1)
    dv = 2 * hd
    nr = qs_ref.shape[1]
    page = k_refs[0].shape[0] // n_heads
    nt_dims = (((1,), (1,)), ((), ()))

    @pl.when(j == 0)
    def _():
        for h in range(n_heads):
            qh = q_ref[:, h * dv:(h + 1) * dv].astype(F32)
            row = lax.broadcasted_iota(jnp.int32, qh.shape, 0)
            lane = lax.broadcasted_iota(jnp.int32, qh.shape, 1)
            keep = (row < nr // 2) == (lane < hd)
            qs_ref[h] = jnp.where(keep, qh, 0.0).astype(BF)
        m_ref[...] = jnp.full(m_ref.shape, -jnp.inf, F32)
        l_ref[...] = jnp.zeros(l_ref.shape, F32)
        acc_ref[...] = jnp.zeros(acc_ref.shape, F32)

    def update(h, s, pv_fn):
        m_prev = m_ref[h]
        m_new = jnp.maximum(m_prev, jnp.max(s, axis=1, keepdims=True))
        alpha = jnp.exp2(m_prev - m_new)
        p = jnp.exp2(s - m_new[:, :1])
        l_ref[h] = alpha * l_ref[h] + jnp.sum(p, axis=1, keepdims=True)
        acc_ref[h] = alpha * acc_ref[h] + pv_fn(p.astype(BF))
        m_ref[h] = m_new

    def head_rows(ref, h):
        return ref[pl.ds(h, page, stride=n_heads), :].astype(BF)

    for h in range(n_heads):
        qs = qs_ref[h]
        s = jnp.concatenate(
            [lax.dot_general(qs, head_rows(k_refs[g], h), nt_dims, preferred_element_type=F32)
             for g in range(n_pg)], axis=1)

        def pv_past(p, h=h):
            out = jnp.dot(p[:, 0:page], head_rows(v_refs[0], h), preferred_element_type=F32)
            for g in range(1, n_pg):
                out += jnp.dot(p[:, g * page:(g + 1) * page], head_rows(v_refs[g], h),
                               preferred_element_type=F32)
            return out

        update(h, s, pv_past)

    @pl.when(j == pl.num_programs(1) - 1)
    def _():
        for h in range(n_heads):
            hs = slice(h * dv, (h + 1) * dv)
            s = lax.dot_general(qs_ref[h], kn_ref[:, hs], nt_dims, preferred_element_type=F32)
            row = lax.broadcasted_iota(jnp.int32, s.shape, 0)
            col = lax.broadcasted_iota(jnp.int32, s.shape, 1)
            s = jnp.where(col <= row % (nr // 2), s, NEG_BIG)
            update(h, s, lambda p, hs=hs: jnp.dot(p, vn_ref[:, hs], preferred_element_type=F32))
            t = acc_ref[h] / l_ref[h]
            o = t[0:nr // 2] - lam_ref[0] * t[nr // 2:nr]
            ms = jnp.mean(o * o, axis=1, keepdims=True)
            o_ref[:, hs] = o * lax.rsqrt(ms + LN_EPS) * g_ref[...] * post_scale


def _attn_sample(page_table_flat, lam, q16, kn, vn, g, cache_k, cache_v, *,
                 layer, n_seq, n_pages, n_heads, hd, post_scale):
    n_pg = math.gcd(16, n_pages)
    nr = 2 * SUBLANES
    dv = 2 * hd
    w = n_heads * dv
    page_rows = cache_k.shape[2]
    page = page_rows // n_heads

    def kv_spec(g):
        return pl.BlockSpec((None, None, page_rows, dv),
                            lambda b, j, pt: (layer, pt[b * n_pages + j * n_pg + g], 0, 0))

    grid_spec = pltpu.PrefetchScalarGridSpec(
        num_scalar_prefetch=1,
        grid=(n_seq, n_pages // n_pg),
        in_specs=[pl.BlockSpec(memory_space=pltpu.SMEM),
                  pl.BlockSpec((nr, w), lambda b, j, pt: (b, 0)),
                  pl.BlockSpec((page, w), lambda b, j, pt: (b, 0)),
                  pl.BlockSpec((page, w), lambda b, j, pt: (b, 0)),
                  pl.BlockSpec((1, dv), lambda b, j, pt: (0, 0))]
                 + [kv_spec(g) for g in range(n_pg)] + [kv_spec(g) for g in range(n_pg)],
        out_specs=pl.BlockSpec((nr // 2, w), lambda b, j, pt: (b, 0)),
        scratch_shapes=[pltpu.VMEM((n_heads, nr, dv), BF), pltpu.VMEM((n_heads, nr, dv), F32),
                        pltpu.VMEM((n_heads, nr, dv), F32), pltpu.VMEM((n_heads, nr, dv), F32)])
    return pl.pallas_call(
        functools.partial(_attn_sample_kernel, n_pg=n_pg, n_heads=n_heads, hd=hd, post_scale=post_scale),
        grid_spec=grid_spec,
        out_shape=jax.ShapeDtypeStruct((n_seq * (nr // 2), w), F32),
        compiler_params=_params("parallel", "arbitrary"),
        name="attn_sample",
    )(page_table_flat, lam, q16, kn, vn, g, *([cache_k] * n_pg), *([cache_v] * n_pg))


def _ssm_input_proj(ub, bw_ref, xre_ref, xim_ref):
    n_slab = bw_ref.shape[0]
    ns = xre_ref.shape[1] // n_slab
    for h in range(n_slab):
        uh = ub[:, h * MXU_K:(h + 1) * MXU_K]
        xre_ref[:, h * ns:(h + 1) * ns] = jnp.dot(uh, bw_ref[h, :, 0:ns], preferred_element_type=F32)
        xim_ref[:, h * ns:(h + 1) * ns] = jnp.dot(uh, bw_ref[h, :, ns:2 * ns], preferred_element_type=F32)


def _ssm_output(u, hre_ref, him_ref, cw_ref, d_ref, gw_ref, gb_ref):
    n_slab = cw_ref.shape[1]
    ns = hre_ref.shape[1] // n_slab
    ys = []
    for h in range(n_slab):
        ss = slice(h * ns, (h + 1) * ns)
        ys.append(jnp.dot(hre_ref[:, ss].astype(BF), cw_ref[0, h], preferred_element_type=F32)
                  + jnp.dot(him_ref[:, ss].astype(BF), cw_ref[1, h], preferred_element_type=F32))
    y = jnp.concatenate(ys, axis=1) + d_ref[...] * u
    a = jax.nn.gelu(y)
    z = jnp.dot(a.astype(BF), gw_ref[...], preferred_element_type=F32) + gb_ref[...]
    return a * jax.nn.sigmoid(z)


def _ssm_prompt_kernel(u_ref, bw_ref, cw_ref, pq_ref, d_ref, gw_ref, gb_ref,
                       out_ref, hre_out, him_out, utb_ref, xre_ref, xim_ref, cre_ref, cim_ref, *, lane_grp):
    @pl.when(pl.program_id(0) == 0)
    def _():
        cre_ref[...] = jnp.zeros(cre_ref.shape, F32)
        cim_ref[...] = jnp.zeros(cim_ref.shape, F32)

    n_seq, t_chunk = u_ref.shape[0], u_ref.shape[1]
    n_slab = utb_ref.shape[0]
    for b in range(n_seq):
        ub = u_ref[b].astype(F32)
        for c in range(n_slab):
            utb_ref[c, pl.ds(b, t_chunk, stride=n_seq), :] = ub[:, c * LANES:(c + 1) * LANES]
    u = jnp.concatenate([utb_ref[c] for c in range(n_slab)], axis=1)
    _ssm_input_proj(u.astype(BF), bw_ref, xre_ref, xim_ref)

    rows, ns = xre_ref.shape
    half = SUBLANES // 2
    for lg in range(ns // lane_grp):
        sl = slice(lg * lane_grp, (lg + 1) * lane_grp)
        p_re, p_im, q_re, q_im = pq_ref[0, :, sl], pq_ref[1, :, sl], pq_ref[2, :, sl], pq_ref[3, :, sl]
        upper = lax.broadcasted_iota(jnp.int32, (SUBLANES, lane_grp), 0) >= half

        def body(i, carry, sl=sl, p_re=p_re, p_im=p_im, q_re=q_re, q_im=q_im, upper=upper):
            c_re, c_im = carry
            r0 = pl.multiple_of(i * SUBLANES, SUBLANES)
            x_re = xre_ref[pl.ds(r0, SUBLANES), sl]
            x_im = xim_ref[pl.ds(r0, SUBLANES), sl]
            s_re = pltpu.roll(x_re, half, 0)
            s_im = pltpu.roll(x_im, half, 0)
            h_re = x_re + q_re * s_re - q_im * s_im + p_re * c_re - p_im * c_im
            h_im = x_im + q_re * s_im + q_im * s_re + p_re * c_im + p_im * c_re
            xre_ref[pl.ds(r0, SUBLANES), sl] = h_re
            xim_ref[pl.ds(r0, SUBLANES), sl] = h_im
            return (jnp.where(upper, h_re, pltpu.roll(h_re, half, 0)),
                    jnp.where(upper, h_im, pltpu.roll(h_im, half, 0)))

        c_re, c_im = lax.fori_loop(0, rows // SUBLANES, body, (cre_ref[:, sl], cim_ref[:, sl]), unroll=2)
        cre_ref[:, sl] = c_re
        cim_ref[:, sl] = c_im

    y = _ssm_output(u, xre_ref, xim_ref, cw_ref, d_ref, gw_ref, gb_ref)
    for c in range(n_slab):
        utb_ref[c] = y[:, c * LANES:(c + 1) * LANES]
    for b in range(n_seq):
        for c in range(n_slab):
            out_ref[b, :, c * LANES:(c + 1) * LANES] = utb_ref[c, pl.ds(b, t_chunk, stride=n_seq), :].astype(BF)
    hre_out[...] = cre_ref[...]
    him_out[...] = cim_ref[...]


def _ssm_prompt(su, bw, cw, pq, d, gw, gb, *, t_chunk):
    n_seq, seq, w = su.shape
    ns = pq.shape[2]
    rows = n_seq * t_chunk
    blk_map = lambda i: (0, i, 0)
    const2 = lambda i: (0, 0)
    return pl.pallas_call(
        functools.partial(_ssm_prompt_kernel, lane_grp=min(512, ns)),
        grid=(seq // t_chunk,),
        in_specs=[pl.BlockSpec((n_seq, t_chunk, w), blk_map),
                  pl.BlockSpec(bw.shape, lambda i: (0, 0, 0)),
                  pl.BlockSpec(cw.shape, lambda i: (0, 0, 0, 0)),
                  pl.BlockSpec(pq.shape, lambda i: (0, 0, 0)),
                  pl.BlockSpec((1, w), const2),
                  pl.BlockSpec(gw.shape, const2),
                  pl.BlockSpec((1, w), const2)],
        out_specs=(pl.BlockSpec((n_seq, t_chunk, w), blk_map),
                   pl.BlockSpec((SUBLANES, ns), const2), pl.BlockSpec((SUBLANES, ns), const2)),
        out_shape=(jax.ShapeDtypeStruct((n_seq, seq, w), BF),
                   jax.ShapeDtypeStruct((SUBLANES, ns), F32), jax.ShapeDtypeStruct((SUBLANES, ns), F32)),
        scratch_shapes=[pltpu.VMEM((w // LANES, rows, LANES), F32),
                        pltpu.VMEM((rows, ns), F32), pltpu.VMEM((rows, ns), F32),
                        pltpu.VMEM((SUBLANES, ns), F32), pltpu.VMEM((SUBLANES, ns), F32)],
        compiler_params=_params("arbitrary"),
        name="ssm_prompt",
    )(su, bw, cw, pq, d, gw, gb)


def _ssm_sample_kernel(u_ref, h0re_ref, h0im_ref, bw_ref, cw_ref, a_ref, d_ref, gw_ref, gb_ref,
                       out_ref, hre_out, him_out, xre_ref, xim_ref, *, n_steps):
    ub = u_ref[...]
    _ssm_input_proj(ub, bw_ref, xre_ref, xim_ref)
    nb = h0re_ref.shape[0]
    a_re, a_im = a_ref[0], a_ref[1]
    h_re, h_im = h0re_ref[...], h0im_ref[...]
    for t in range(n_steps):
        rs = slice(t * nb, (t + 1) * nb)
        h_re, h_im = (a_re * h_re - a_im * h_im + xre_ref[rs, :],
                      a_re * h_im + a_im * h_re + xim_ref[rs, :])
        xre_ref[rs, :] = h_re
        xim_ref[rs, :] = h_im
    out_ref[...] = _ssm_output(ub.astype(F32), xre_ref, xim_ref, cw_ref, d_ref, gw_ref, gb_ref).astype(BF)
    hre_out[...] = h_re
    him_out[...] = h_im


def _ssm_sample(su, h0_re, h0_im, bw, cw, a, d, gw, gb, *, n_steps):
    rows, w = su.shape
    nb, ns = h0_re.shape
    full = lambda arr: pl.BlockSpec(arr.shape, lambda i, n=arr.ndim: (0,) * n)
    return pl.pallas_call(
        functools.partial(_ssm_sample_kernel, n_steps=n_steps),
        grid=(1,),
        in_specs=[full(su), full(h0_re), full(h0_im), full(bw), full(cw), full(a), full(d), full(gw), full(gb)],
        out_specs=(pl.BlockSpec((rows, w), lambda i: (0, 0)),
                   pl.BlockSpec((nb, ns), lambda i: (0, 0)), pl.BlockSpec((nb, ns), lambda i: (0, 0))),
        out_shape=(jax.ShapeDtypeStruct((rows, w), BF),
                   jax.ShapeDtypeStruct((nb, ns), F32), jax.ShapeDtypeStruct((nb, ns), F32)),
        scratch_shapes=[pltpu.VMEM((rows, ns), F32), pltpu.VMEM((rows, ns), F32)],
        compiler_params=_params("arbitrary"),
        name="ssm_sample",
    )(su, h0_re, h0_im, bw, cw, a, d, gw, gb)


def _merge_kernel(at_ref, gm_ref, ss_ref, gates_ref, x_ref, wb_ref, wo_ref, lg_ref, lb_ref, o_ref, *, alpha):
    d = x_ref.shape[1]
    merged = None
    for n, br in enumerate((at_ref, gm_ref, ss_ref)):
        proj = jnp.dot(br[...], wb_ref[n], preferred_element_type=F32)
        gated = gates_ref[:, n * d:(n + 1) * d].astype(F32) * proj
        merged = gated if merged is None else merged + gated
    mix = jnp.dot(merged.astype(BF), wo_ref[...], preferred_element_type=F32)
    o_ref[...] = _layer_norm(alpha * x_ref[...] + mix, lg_ref[...], lb_ref[...])


def _merge(attn, gm, ssm, gates, x, wb_b, wo_b, ln_g, ln_b, *, tm, alpha):
    R, d = x.shape
    w = attn.shape[1]
    row_map = lambda i: (i, 0)
    const2 = lambda i: (0, 0)
    return pl.pallas_call(
        functools.partial(_merge_kernel, alpha=alpha),
        grid=(R // tm,),
        in_specs=[pl.BlockSpec((tm, w), row_map), pl.BlockSpec((tm, w), row_map),
                  pl.BlockSpec((tm, w), row_map),
                  pl.BlockSpec((tm, gates.shape[1]), row_map),
                  pl.BlockSpec((tm, d), row_map),
                  pl.BlockSpec(wb_b.shape, lambda i: (0, 0, 0)),
                  pl.BlockSpec(wo_b.shape, const2),
                  pl.BlockSpec((1, d), const2), pl.BlockSpec((1, d), const2)],
        out_specs=pl.BlockSpec((tm, d), row_map),
        out_shape=jax.ShapeDtypeStruct((R, d), F32),
        compiler_params=_params("parallel"),
        name="merge",
    )(attn, gm, ssm, gates, x, wb_b, wo_b, ln_g, ln_b)


def _ffn_kernel(x_ref, wg_ref, wu_ref, wo_ref, lg_ref, lb_ref, o_ref, xb_ref, acc_ref, *, alpha):
    j = pl.program_id(1)

    @pl.when(j == 0)
    def _():
        xb_ref[...] = x_ref[...].astype(BF)
        acc_ref[...] = jnp.zeros(acc_ref.shape, F32)

    xb = xb_ref[...]
    gate = jnp.dot(xb, wg_ref[...], preferred_element_type=F32)
    up = jnp.dot(xb, wu_ref[...], preferred_element_type=F32)
    hid = (gate * jax.nn.sigmoid(gate) * up).astype(BF)
    acc_ref[...] += jnp.dot(hid, wo_ref[...], preferred_element_type=F32)

    @pl.when(j == pl.num_programs(1) - 1)
    def _():
        o_ref[...] = _layer_norm(alpha * x_ref[...] + acc_ref[...], lg_ref[...], lb_ref[...])


def _ffn(x, w_in_b, w_out_b, ln_g, ln_b, *, tm, tf, alpha):
    R, d = x.shape
    d_ff = w_out_b.shape[0]
    nf = d_ff // tf
    row_map = lambda i, j: (i, 0)
    const2 = lambda i, j: (0, 0)
    return pl.pallas_call(
        functools.partial(_ffn_kernel, alpha=alpha),
        grid=(R // tm, nf),
        in_specs=[pl.BlockSpec((tm, d), row_map),
                  pl.BlockSpec((d, tf), lambda i, j: (0, j)),
                  pl.BlockSpec((d, tf), lambda i, j: (0, j + nf)),
                  pl.BlockSpec((tf, d), lambda i, j: (j, 0)),
                  pl.BlockSpec((1, d), const2), pl.BlockSpec((1, d), const2)],
        out_specs=pl.BlockSpec((tm, d), row_map),
        out_shape=jax.ShapeDtypeStruct((R, d), F32),
        scratch_shapes=[pltpu.VMEM((tm, d), BF), pltpu.VMEM((tm, d), F32)],
        compiler_params=_params("parallel", "arbitrary"),
        name="ffn",
    )(x, w_in_b, w_in_b, w_out_b, ln_g, ln_b)


def _rope_tables(pos, hd):
    half = hd // 2
    inv = ROPE_THETA ** (-jnp.arange(half, dtype=F32) / half)
    ang = pos.astype(F32)[:, None] * inv[None, :]
    cos, sin = jnp.cos(ang), jnp.sin(ang)
    reps = LANES // hd
    return (jnp.tile(jnp.concatenate([cos, cos], axis=1), (1, reps)),
            jnp.tile(jnp.concatenate([-sin, sin], axis=1), (1, reps)))


def _ssm_discretize(a_re, a_im, log_dt, b_re, b_im):
    dt = jnp.exp(log_dt.astype(F32))[:, None]
    ar, ai = a_re.astype(F32), a_im.astype(F32)
    mag = jnp.exp(dt * ar)
    abar_re = mag * jnp.cos(dt * ai)
    abar_im = mag * jnp.sin(dt * ai)
    nr, ni = abar_re - 1.0, abar_im
    den = ar * ar + ai * ai
    c_re = (nr * ar + ni * ai) / den
    c_im = (ni * ar - nr * ai) / den
    br, bi = b_re.astype(F32), b_im.astype(F32)
    bbar_re = c_re[..., None] * br - c_im[..., None] * bi
    bbar_im = c_re[..., None] * bi + c_im[..., None] * br
    return abar_re, abar_im, bbar_re, bbar_im


def _ssm_weights(a_re, a_im, log_dt, b_re, b_im, c_re, c_im):
    n_grp, n_state, grp_ch = b_re.shape
    abar_re, abar_im, bbar_re, bbar_im = _ssm_discretize(a_re, a_im, log_dt, b_re, b_im)
    gps = MXU_K // grp_ch
    n_slab = n_grp // gps
    eye = jnp.eye(gps, dtype=F32)

    def in_blocks(bb):
        bb = bb.reshape(n_slab, gps, n_state, grp_ch)
        return jnp.einsum('sgpc,gk->sgckp', bb, eye).reshape(n_slab, gps * grp_ch, gps * n_state)

    def out_blocks(cc):
        cc = cc.reshape(n_slab, gps, grp_ch, n_state)
        return jnp.einsum('sgcp,gk->sgpkc', cc, eye).reshape(n_slab, gps * n_state, gps * grp_ch)

    bw = jnp.concatenate([in_blocks(bbar_re), in_blocks(bbar_im)], axis=2).astype(BF)
    cw = jnp.stack([out_blocks(c_re.astype(F32)), -out_blocks(c_im.astype(F32))]).astype(BF)
    a_flat = jnp.stack([abar_re.reshape(1, -1), abar_im.reshape(1, -1)])
    a2_re = abar_re * abar_re - abar_im * abar_im
    a2_im = 2.0 * abar_re * abar_im
    half = SUBLANES // 2

    def halves(lo, hi):
        return jnp.concatenate([jnp.broadcast_to(lo.reshape(1, -1), (half, lo.size)),
                                jnp.broadcast_to(hi.reshape(1, -1), (half, hi.size))], axis=0)

    zero = jnp.zeros_like(abar_re)
    pq = jnp.stack([halves(abar_re, a2_re), halves(abar_im, a2_im),
                    halves(zero, abar_re), halves(zero, abar_im)])
    return bw, cw, a_flat, pq


def _lambda_init(l):
    return 0.8 - 0.6 * math.exp(-0.3 * l)


def kernel(x_prompt, x_sample, cache_k, cache_v, page_table, state_ssm_re, state_ssm_im, w_in, b_gate, lambda_q1, lambda_k1, lambda_q2, lambda_k2, attn_norm_g, gmlp_ln_g, gmlp_ln_b, gmlp_ws, gmlp_bs, ssm_a_re, ssm_a_im, ssm_log_dt, ssm_b_re, ssm_b_im, ssm_c_re, ssm_c_im, ssm_d, ssm_glu_w, ssm_glu_b, w_branch, w_out, ln1_g, ln1_b, ln2_g, ln2_b, w_ffn_in, w_ffn_out):
    bp, lp, d_model = x_prompt.shape
    bs, ls, _ = x_sample.shape
    depth = w_in.shape[0]
    n_pool, page, n_heads, dv = cache_k.shape[1:]
    hd = dv // 2
    n_pages = page_table.shape[1]
    n_past = n_pages * page
    wbr = ssm_glu_w.shape[1]
    n_grp_g, chunk = gmlp_ws.shape[1], gmlp_ws.shape[2]
    n_grp_s, n_state, grp_ch = ssm_b_re.shape[1:]
    ns = n_grp_s * n_state
    d_ff = w_ffn_out.shape[1]
    alpha = (2 * depth) ** 0.25
    rp, rs = bp * lp, bs * ls
    assert bp == SUBLANES // 2 and n_heads * dv == wbr and wbr % MXU_K == 0 and LANES % hd == 0
    assert dv == LANES and ls <= SUBLANES and rs == chunk and lp % page == 0

    xp = x_prompt.reshape(rp, d_model)
    xs = x_sample.transpose(1, 0, 2).reshape(rs, d_model)
    cos_p, sin_p = _rope_tables(jnp.arange(lp), hd)
    cos_s, sin_s = _rope_tables(jnp.repeat(n_past + jnp.arange(ls), bs), hd)
    ck = cache_k.reshape(depth, n_pool, page * n_heads, dv)
    cv = cache_v.reshape(depth, n_pool, page * n_heads, dv)
    pt_flat = page_table.reshape(-1).astype(jnp.int32)

    tm_p = min(1024, lp)
    tk = min(512, lp)
    tf = 256 if d_ff % 256 == 0 else d_ff
    t_chunk = min(256, lp)
    gc = wbr // n_grp_g

    outs = {k: [] for k in ("kp", "vp", "ks", "vs", "hrp", "hip", "hrs", "his", "gvs")}
    for l in range(depth):
        lam_init = _lambda_init(l)
        post_scale = 1.0 - lam_init
        lam = (jnp.exp(jnp.sum(lambda_q1[l].astype(F32) * lambda_k1[l].astype(F32)))
               - jnp.exp(jnp.sum(lambda_q2[l].astype(F32) * lambda_k2[l].astype(F32)))
               + lam_init).reshape(1)
        w_in_b = w_in[l].astype(BF)
        wb_b = w_branch[l].astype(BF)
        wo_b = w_out[l].astype(BF)
        wfi_b = w_ffn_in[l].astype(BF)
        wfo_b = w_ffn_out[l].astype(BF)
        glu_b16 = ssm_glu_w[l].astype(BF)
        an_g = attn_norm_g[l].reshape(1, dv)
        ws_tril = jnp.tril(gmlp_ws[l])
        bw, cw, a_flat, pq = _ssm_weights(ssm_a_re[l], ssm_a_im[l], ssm_log_dt[l], ssm_b_re[l], ssm_b_im[l],
                                          ssm_c_re[l], ssm_c_im[l])
        d_row = ssm_d[l].reshape(1, wbr)
        glu_bias = ssm_glu_b[l].reshape(1, wbr)
        ln_gg, ln_gb = gmlp_ln_g[l].reshape(1, wbr), gmlp_ln_b[l].reshape(1, wbr)
        bgate = b_gate[l].reshape(1, -1)
        l1g, l1b = ln1_g[l].reshape(1, d_model), ln1_b[l].reshape(1, d_model)
        l2g, l2b = ln2_g[l].reshape(1, d_model), ln2_b[l].reshape(1, d_model)

        bias_p = jnp.repeat(gmlp_bs[l].T, gc, axis=1)
        q, kt, kf, vx, vf, gm, su, gates = _in_proj(
            xp, w_in_b, cos_p, sin_p, bgate, ln_gg, ln_gb, ws_tril.astype(BF), bias_p,
            wb=wbr, hd=hd, n_heads=n_heads, tm=tm_p, tk=tk, rows_per_seq=lp, prompt=True)
        attn = _attn_prompt(lam, q, kt, vx, an_g, n_seq=bp, seq=lp, n_heads=n_heads, hd=hd, tk=tk,
                            post_scale=post_scale)
        ssm, hre, him = _ssm_prompt(su.reshape(bp, lp, wbr), bw, cw, pq, d_row, glu_b16, glu_bias,
                                    t_chunk=t_chunk)
        x1 = _merge(attn, gm, ssm.reshape(rp, wbr), gates, xp, wb_b, wo_b, l1g, l1b,
                    tm=min(512, rp), alpha=alpha)
        xp = _ffn(x1, wfi_b, wfo_b, l2g, l2b, tm=min(1024, rp), tf=tf, alpha=alpha)
        outs["kp"].append(kf.reshape(bp, lp // page, page, n_heads, dv))
        outs["vp"].append(vf.reshape(bp, lp // page, page, n_heads, dv))
        outs["hrp"].append(hre[SUBLANES // 2:].reshape(bp, n_grp_s, n_state))
        outs["hip"].append(him[SUBLANES // 2:].reshape(bp, n_grp_s, n_state))

        ws_s = jnp.einsum('gts,bc->gtbsc', ws_tril[:, :ls, :ls], jnp.eye(bs, dtype=F32)).reshape(n_grp_g, rs, rs)
        bias_s = jnp.repeat(jnp.repeat(gmlp_bs[l][:, :ls].T, bs, axis=0), gc, axis=1)
        q, kb, kf, vb, vf, gm, vn, su, gates = _in_proj(
            xs, w_in_b, cos_s, sin_s, bgate, ln_gg, ln_gb, ws_s.astype(BF), bias_s,
            wb=wbr, hd=hd, n_heads=n_heads, tm=rs, tk=rs, rows_per_seq=rs, prompt=False)
        q3 = q.reshape(ls, bs, wbr).transpose(1, 0, 2)
        zq = jnp.zeros((bs, SUBLANES - ls, wbr), BF)
        q16 = jnp.concatenate([q3, zq, q3, zq], axis=1).reshape(bs * 2 * SUBLANES, wbr)
        pad_new = lambda a: jnp.pad(a.reshape(ls, bs, wbr).transpose(1, 0, 2),
                                    ((0, 0), (0, page - ls), (0, 0))).reshape(bs * page, wbr)
        attn8 = _attn_sample(pt_flat, lam, q16, pad_new(kb), pad_new(vb), an_g, ck, cv,
                             layer=l, n_seq=bs, n_pages=n_pages, n_heads=n_heads, hd=hd,
                             post_scale=post_scale)
        attn = attn8.reshape(bs, SUBLANES, wbr)[:, :ls].transpose(1, 0, 2).reshape(rs, wbr).astype(BF)
        ssm, hre, him = _ssm_sample(su, state_ssm_re[l].reshape(bs, ns), state_ssm_im[l].reshape(bs, ns),
                                    bw, cw, a_flat, d_row, glu_b16, glu_bias, n_steps=ls)
        x1 = _merge(attn, gm, ssm, gates, xs, wb_b, wo_b, l1g, l1b, tm=rs, alpha=alpha)
        xs = _ffn(x1, wfi_b, wfo_b, l2g, l2b, tm=rs, tf=tf, alpha=alpha)
        to_seq_major = lambda a: a.reshape(ls, bs, -1).transpose(1, 0, 2)
        outs["ks"].append(to_seq_major(kf).reshape(bs, ls, n_heads, dv))
        outs["vs"].append(to_seq_major(vf).reshape(bs, ls, n_heads, dv))
        outs["hrs"].append(hre.reshape(bs, n_grp_s, n_state))
        outs["his"].append(him.reshape(bs, n_grp_s, n_state))
        outs["gvs"].append(to_seq_major(vn))

    st = lambda k: jnp.stack(outs[k])
    return (xp.reshape(bp, lp, d_model), xs.reshape(ls, bs, d_model).transpose(1, 0, 2),
            st("kp"), st("vp"), st("ks"), st("vs"),
            st("hrp"), st("hip"), st("hrs"), st("his"), st("gvs"))
```

```python
import functools
import math

import jax
import jax.numpy as jnp
from jax import lax
from jax.experimental import pallas as pl
from jax.experimental.pallas import tpu as pltpu

BF = jnp.bfloat16
F32 = jnp.float32

LN_EPS = 1e-5
ROPE_THETA = 10000.0
NEG_BIG = -1e30
LOG2E = math.log2(math.e)
LANES = 128
SUBLANES = 8
MXU_K = 256
VMEM_LIMIT = 56 * 1024 * 1024


def _params(*sem):
    return pltpu.CompilerParams(dimension_semantics=sem, vmem_limit_bytes=VMEM_LIMIT)


def _layer_norm(r, g, b):
    mu = jnp.mean(r, axis=-1, keepdims=True)
    d = r - mu
    var = jnp.mean(d * d, axis=-1, keepdims=True)
    return d * lax.rsqrt(var + LN_EPS) * g + b


def _in_proj_kernel(*refs, hd, n_heads, q_scale, prompt, tk, n_alias):
    x_ref, w_ref, cos_ref, sin_ref, bg_ref, lng_ref, lnb_ref, ws_ref, bias_ref = refs[:9]
    rest = refs[9 + n_alias:]
    if prompt:
        q_ref, kt_ref, kf_ref, vx_ref, vf_ref, gm_ref, su_ref, gates_ref = rest
    else:
        q_ref, kb_ref, kf_ref, vb_ref, vf_ref, gm_ref, vn_ref, su_ref, gates_ref = rest
    tm = x_ref.shape[0]
    wb = q_ref.shape[1]
    dv = 2 * hd
    half = hd // 2
    xb = x_ref[...].astype(BF)

    def col(c):
        return jnp.dot(xb, w_ref[:, c * wb:(c + 1) * wb], preferred_element_type=F32)

    def rope(a):
        cos = cos_ref[...]
        sin = sin_ref[...]
        outs = []
        for c in range(a.shape[1] // LANES):
            blk = a[:, c * LANES:(c + 1) * LANES]
            lane = lax.broadcasted_iota(jnp.int32, blk.shape, 1)
            swapped = jnp.where((lane % hd) < half,
                                pltpu.roll(blk, LANES - half, 1), pltpu.roll(blk, half, 1))
            outs.append(blk * cos + swapped * sin)
        return jnp.concatenate(outs, axis=1)

    def store_head_rows(dst_ref, a):
        for h in range(n_heads):
            dst_ref[pl.ds(h, tm, stride=n_heads), :] = a[:, h * dv:(h + 1) * dv]

    q_ref[...] = (rope(col(0)) * q_scale).astype(BF)

    k = rope(col(1))
    if prompt:
        store_head_rows(kf_ref, k)
        for t in range(tm // tk):
            kt_ref[t] = k[t * tk:(t + 1) * tk, :].T.astype(BF)
    else:
        kf_ref[...] = k
        kb_ref[...] = k.astype(BF)

    v = col(2)
    if prompt:
        store_head_rows(vf_ref, v)
        ones = jnp.ones((tm, dv), BF)
        for h in range(n_heads):
            vx_ref[:, 2 * h * dv:(2 * h + 1) * dv] = v[:, h * dv:(h + 1) * dv].astype(BF)
            vx_ref[:, (2 * h + 1) * dv:(2 * h + 2) * dv] = ones
    else:
        vf_ref[...] = v
        vb_ref[...] = v.astype(BF)

    vn = _layer_norm(jax.nn.gelu(col(4)), lng_ref[...], lnb_ref[...])
    if not prompt:
        vn_ref[...] = vn
    vb16 = vn.astype(BF)
    u = jax.nn.gelu(col(3))
    n_grp, chunk = ws_ref.shape[0], ws_ref.shape[1]
    gc = wb // n_grp
    for n in range(tm // chunk):
        rs = slice(n * chunk, (n + 1) * chunk)
        for g in range(n_grp):
            cs = slice(g * gc, (g + 1) * gc)
            mixed = jnp.dot(ws_ref[g], vb16[rs, cs], preferred_element_type=F32) + bias_ref[:, cs]
            gm_ref[rs, cs] = (u[rs, cs] * mixed).astype(BF)

    su_ref[...] = col(5).astype(BF)

    for n in range(gates_ref.shape[1] // wb):
        cs = slice(n * wb, (n + 1) * wb)
        gates_ref[:, cs] = jax.nn.sigmoid(col(6 + n) + bg_ref[:, cs]).astype(BF)


def _in_proj(x, w_b, cos_t, sin_t, b_gate, ln_g, ln_b, ws_b, bias_full, kv_prev=(), *,
             wb, hd, n_heads, tm, tk, rows_per_seq, prompt, layer=0, depth=1):
    R, D = x.shape
    n_gate = w_b.shape[1] // wb - 6
    nt = rows_per_seq // tm
    dv = 2 * hd
    chunk = ws_b.shape[1]
    row_map = lambda i: (i, 0)
    const2 = lambda i: (0, 0)
    tab_map = lambda i: (i % nt, 0)
    blk = lambda n=1: pl.BlockSpec((tm, n * wb), row_map)
    sds = jax.ShapeDtypeStruct
    if prompt:
        stacked = sds((depth, R * n_heads, dv), F32)
        stacked_spec = pl.BlockSpec((None, tm * n_heads, dv), lambda i: (layer, i, 0))
        out_shape = (sds((R, wb), BF), sds((R // tk, wb, tk), BF), stacked, sds((R, 2 * wb), BF), stacked,
                     sds((R, wb), BF), sds((R, wb), BF), sds((R, n_gate * wb), BF))
        out_specs = (blk(), pl.BlockSpec((tm // tk, wb, tk), lambda i: (i, 0, 0)), stacked_spec,
                     blk(2), stacked_spec, blk(), blk(), blk(n_gate))
        aliases = {9 + n: (2, 4)[n] for n in range(len(kv_prev))}
    else:
        out_shape = (sds((R, wb), BF), sds((R, wb), BF), sds((R, wb), F32), sds((R, wb), BF), sds((R, wb), F32),
                     sds((R, wb), BF), sds((R, wb), F32), sds((R, wb), BF), sds((R, n_gate * wb), BF))
        out_specs = (blk(),) * 8 + (blk(n_gate),)
        aliases = {}
    return pl.pallas_call(
        functools.partial(_in_proj_kernel, hd=hd, n_heads=n_heads, q_scale=hd ** -0.5 * LOG2E,
                          prompt=prompt, tk=tk, n_alias=len(kv_prev)),
        grid=(R // tm,),
        in_specs=[pl.BlockSpec((tm, D), row_map),
                  pl.BlockSpec(w_b.shape, const2, pipeline_mode=pl.Buffered(1)),
                  pl.BlockSpec((tm, LANES), tab_map),
                  pl.BlockSpec((tm, LANES), tab_map),
                  pl.BlockSpec((1, n_gate * wb), const2),
                  pl.BlockSpec((1, wb), const2), pl.BlockSpec((1, wb), const2),
                  pl.BlockSpec(ws_b.shape, lambda i: (0, 0, 0)),
                  pl.BlockSpec((chunk, wb), const2)]
                 + [pl.BlockSpec(memory_space=pl.ANY)] * len(kv_prev),
        out_specs=out_specs,
        out_shape=out_shape,
        input_output_aliases=aliases,
        compiler_params=_params("parallel"),
        name="in_proj_prompt" if prompt else "in_proj_sample",
    )(x, w_b, cos_t, sin_t, b_gate, ln_g, ln_b, ws_b, bias_full, *kv_prev)


def _attn_prompt_kernel(lam_ref, q_ref, kt_ref, vx_ref, g_ref, o_ref, qs_ref, m_ref, acc_ref,
                        *, tq, tk, hd, n_heads, post_scale):
    qi = pl.program_id(1)
    dv = 2 * hd
    for h in range(n_heads):
        q = q_ref[:, h * dv:(h + 1) * dv].astype(F32)
        lane = lax.broadcasted_iota(jnp.int32, q.shape, 1)
        qs_ref[h, 0:tq, :] = jnp.where(lane < hd, q, 0.0).astype(BF)
        qs_ref[h, tq:2 * tq, :] = jnp.where(lane >= hd, q, 0.0).astype(BF)
    m_ref[...] = jnp.full(m_ref.shape, -jnp.inf, F32)
    acc_ref[...] = jnp.zeros(acc_ref.shape, F32)

    def tile(ki, masked):
        k0 = pl.multiple_of(ki * tk, tk)
        for h in range(n_heads):
            s = jnp.dot(qs_ref[h], kt_ref[ki, h * dv:(h + 1) * dv, :], preferred_element_type=F32)
            if masked:
                row = lax.broadcasted_iota(jnp.int32, s.shape, 0)
                col = lax.broadcasted_iota(jnp.int32, s.shape, 1)
                s = jnp.where(ki * tk + col <= qi * tq + row % tq, s, NEG_BIG)
            m_prev = m_ref[h]
            m_new = jnp.maximum(m_prev, jnp.max(s, axis=1, keepdims=True))
            alpha = jnp.exp2(m_prev - m_new)
            p = jnp.exp2(s - m_new[:, :1]).astype(BF)
            pv = jnp.dot(p, vx_ref[pl.ds(k0, tk), 2 * h * dv:(2 * h + 2) * dv], preferred_element_type=F32)
            acc_ref[h, :, 0:dv] = alpha * acc_ref[h, :, 0:dv] + pv[:, 0:dv]
            acc_ref[h, :, dv:2 * dv] = alpha * acc_ref[h, :, dv:2 * dv] + pv[:, dv:2 * dv]
            m_ref[h] = m_new

    n_full = (qi * tq) // tk

    def full_tile(ki, carry):
        tile(ki, False)
        return carry

    lax.fori_loop(0, n_full, full_tile, 0)
    tile(n_full, True)

    for h in range(n_heads):
        t = acc_ref[h, :, 0:dv] / acc_ref[h, :, dv:2 * dv]
        o = t[0:tq] - lam_ref[0] * t[tq:2 * tq]
        ms = jnp.mean(o * o, axis=1, keepdims=True)
        o_ref[:, h * dv:(h + 1) * dv] = (o * lax.rsqrt(ms + LN_EPS) * g_ref[...] * post_scale).astype(BF)


def _attn_prompt(lam, q, kt, vx, g, *, n_seq, seq, n_heads, hd, tk, post_scale):
    tq = min(256, seq)
    dv = 2 * hd
    w = n_heads * dv
    nq = seq // tq
    return pl.pallas_call(
        functools.partial(_attn_prompt_kernel, tq=tq, tk=tk, hd=hd, n_heads=n_heads, post_scale=post_scale),
        grid=(n_seq, nq),
        in_specs=[pl.BlockSpec(memory_space=pltpu.SMEM),
                  pl.BlockSpec((tq, w), lambda b, i: (b * nq + i, 0)),
                  pl.BlockSpec((seq // tk, w, tk), lambda b, i: (b, 0, 0)),
                  pl.BlockSpec((seq, 2 * w), lambda b, i: (b, 0)),
                  pl.BlockSpec((1, dv), lambda b, i: (0, 0))],
        out_specs=pl.BlockSpec((tq, w), lambda b, i: (b * nq + i, 0)),
        out_shape=jax.ShapeDtypeStruct(q.shape, BF),
        scratch_shapes=[pltpu.VMEM((n_heads, 2 * tq, dv), BF), pltpu.VMEM((n_heads, 2 * tq, dv), F32),
                        pltpu.VMEM((n_heads, 2 * tq, 2 * dv), F32)],
        compiler_params=_params("parallel", "arbitrary"),
        name="attn_prompt",
    )(lam, q, kt, vx, g)


def _attn_sample_kernel(pt_ref, lam_ref, q_ref, kn_ref, vn_ref, g_ref, *rest,
                        n_pg, n_heads, hd, post_scale):
    k_refs = rest[:n_pg]
    v_refs = rest[n_pg:2 * n_pg]
    o_ref, qs_ref, m_ref, l_ref, acc_ref = rest[2 * n_pg:]
    j = pl.program_id(1)
    dv = 2 * hd
    nr = qs_ref.shape[1]
    page = k_refs[0].shape[0] // n_heads
    nt_dims = (((1,), (1,)), ((), ()))

    @pl.when(j == 0)
    def _():
        for h in range(n_heads):
            qh = q_ref[:, h * dv:(h + 1) * dv].astype(F32)
            row = lax.broadcasted_iota(jnp.int32, qh.shape, 0)
            lane = lax.broadcasted_iota(jnp.int32, qh.shape, 1)
            keep = (row < nr // 2) == (lane < hd)
            qs_ref[h] = jnp.where(keep, qh, 0.0).astype(BF)
        m_ref[...] = jnp.full(m_ref.shape, -jnp.inf, F32)
        l_ref[...] = jnp.zeros(l_ref.shape, F32)
        acc_ref[...] = jnp.zeros(acc_ref.shape, F32)

    def update(h, s, pv_fn):
        m_prev = m_ref[h]
        m_new = jnp.maximum(m_prev, jnp.max(s, axis=1, keepdims=True))
        alpha = jnp.exp2(m_prev - m_new)
        p = jnp.exp2(s - m_new[:, :1])
        l_ref[h] = alpha * l_ref[h] + jnp.sum(p, axis=1, keepdims=True)
        acc_ref[h] = alpha * acc_ref[h] + pv_fn(p.astype(BF))
        m_ref[h] = m_new

    def head_rows(page_refs, h):
        return jnp.concatenate([r[pl.ds(h, page, stride=n_heads), :] for r in page_refs], axis=0).astype(BF)

    for h in range(n_heads):
        s = lax.dot_general(qs_ref[h], head_rows(k_refs, h), nt_dims, preferred_element_type=F32)
        update(h, s, lambda p, h=h: jnp.dot(p, head_rows(v_refs, h), preferred_element_type=F32))

    @pl.when(j == pl.num_programs(1) - 1)
    def _():
        for h in range(n_heads):
            hs = slice(h * dv, (h + 1) * dv)
            s = lax.dot_general(qs_ref[h], kn_ref[:, hs], nt_dims, preferred_element_type=F32)
            row = lax.broadcasted_iota(jnp.int32, s.shape, 0)
            col = lax.broadcasted_iota(jnp.int32, s.shape, 1)
            s = jnp.where(col <= row % (nr // 2), s, NEG_BIG)
            update(h, s, lambda p, hs=hs: jnp.dot(p, vn_ref[:, hs], preferred_element_type=F32))
            t = acc_ref[h] / l_ref[h]
            o = t[0:nr // 2] - lam_ref[0] * t[nr // 2:nr]
            ms = jnp.mean(o * o, axis=1, keepdims=True)
            o_ref[:, hs] = o * lax.rsqrt(ms + LN_EPS) * g_ref[...] * post_scale


def _attn_sample(page_table_flat, lam, q16, kn, vn, g, cache_k, cache_v, *,
                 layer, n_seq, n_pages, n_heads, hd, post_scale):
    n_pg = math.gcd(16, n_pages)
    nr = 2 * SUBLANES
    dv = 2 * hd
    w = n_heads * dv
    page_rows = cache_k.shape[2]
    page = page_rows // n_heads

    def kv_spec(g):
        return pl.BlockSpec((None, None, page_rows, dv),
                            lambda b, j, pt: (layer, pt[b * n_pages + j * n_pg + g], 0, 0))

    grid_spec = pltpu.PrefetchScalarGridSpec(
        num_scalar_prefetch=1,
        grid=(n_seq, n_pages // n_pg),
        in_specs=[pl.BlockSpec(memory_space=pltpu.SMEM),
                  pl.BlockSpec((nr, w), lambda b, j, pt: (b, 0)),
                  pl.BlockSpec((page, w), lambda b, j, pt: (b, 0)),
                  pl.BlockSpec((page, w), lambda b, j, pt: (b, 0)),
                  pl.BlockSpec((1, dv), lambda b, j, pt: (0, 0))]
                 + [kv_spec(g) for g in range(n_pg)] + [kv_spec(g) for g in range(n_pg)],
        out_specs=pl.BlockSpec((nr // 2, w), lambda b, j, pt: (b, 0)),
        scratch_shapes=[pltpu.VMEM((n_heads, nr, dv), BF), pltpu.VMEM((n_heads, nr, dv), F32),
                        pltpu.VMEM((n_heads, nr, dv), F32), pltpu.VMEM((n_heads, nr, dv), F32)])
    return pl.pallas_call(
        functools.partial(_attn_sample_kernel, n_pg=n_pg, n_heads=n_heads, hd=hd, post_scale=post_scale),
        grid_spec=grid_spec,
        out_shape=jax.ShapeDtypeStruct((n_seq * (nr // 2), w), F32),
        compiler_params=_params("parallel", "arbitrary"),
        name="attn_sample",
    )(page_table_flat, lam, q16, kn, vn, g, *([cache_k] * n_pg), *([cache_v] * n_pg))


def _ssm_input_proj(ub, bw_ref, xre_ref, xim_ref):
    n_slab = bw_ref.shape[0]
    ns = xre_ref.shape[1] // n_slab
    for h in range(n_slab):
        uh = ub[:, h * MXU_K:(h + 1) * MXU_K]
        xre_ref[:, h * ns:(h + 1) * ns] = jnp.dot(uh, bw_ref[h, :, 0:ns], preferred_element_type=F32)
        xim_ref[:, h * ns:(h + 1) * ns] = jnp.dot(uh, bw_ref[h, :, ns:2 * ns], preferred_element_type=F32)


def _ssm_output(u, hre_ref, him_ref, cw_ref, d_ref, gw_ref, gb_ref):
    n_slab = cw_ref.shape[1]
    ns = hre_ref.shape[1] // n_slab
    ys = []
    for h in range(n_slab):
        ss = slice(h * ns, (h + 1) * ns)
        ys.append(jnp.dot(hre_ref[:, ss].astype(BF), cw_ref[0, h], preferred_element_type=F32)
                  + jnp.dot(him_ref[:, ss].astype(BF), cw_ref[1, h], preferred_element_type=F32))
    y = jnp.concatenate(ys, axis=1) + d_ref[...] * u
    a = jax.nn.gelu(y)
    z = jnp.dot(a.astype(BF), gw_ref[...], preferred_element_type=F32) + gb_ref[...]
    return a * jax.nn.sigmoid(z)


def _ssm_prompt_kernel(u_ref, bw_ref, cw_ref, pq_ref, d_ref, gw_ref, gb_ref,
                       out_ref, hre_out, him_out, utb_ref, xre_ref, xim_ref, cre_ref, cim_ref, *, lane_grp):
    @pl.when(pl.program_id(0) == 0)
    def _():
        cre_ref[...] = jnp.zeros(cre_ref.shape, F32)
        cim_ref[...] = jnp.zeros(cim_ref.shape, F32)

    n_seq, t_chunk = u_ref.shape[0], u_ref.shape[1]
    n_slab = utb_ref.shape[0]
    for b in range(n_seq):
        ub = u_ref[b].astype(F32)
        for c in range(n_slab):
            utb_ref[c, pl.ds(b, t_chunk, stride=n_seq), :] = ub[:, c * LANES:(c + 1) * LANES]
    u = jnp.concatenate([utb_ref[c] for c in range(n_slab)], axis=1)
    _ssm_input_proj(u.astype(BF), bw_ref, xre_ref, xim_ref)

    rows, ns = xre_ref.shape
    half = SUBLANES // 2
    for lg in range(ns // lane_grp):
        sl = slice(lg * lane_grp, (lg + 1) * lane_grp)
        p_re, p_im, q_re, q_im = pq_ref[0, :, sl], pq_ref[1, :, sl], pq_ref[2, :, sl], pq_ref[3, :, sl]
        upper = lax.broadcasted_iota(jnp.int32, (SUBLANES, lane_grp), 0) >= half

        def body(i, carry, sl=sl, p_re=p_re, p_im=p_im, q_re=q_re, q_im=q_im, upper=upper):
            c_re, c_im = carry
            r0 = pl.multiple_of(i * SUBLANES, SUBLANES)
            x_re = xre_ref[pl.ds(r0, SUBLANES), sl]
            x_im = xim_ref[pl.ds(r0, SUBLANES), sl]
            s_re = pltpu.roll(x_re, half, 0)
            s_im = pltpu.roll(x_im, half, 0)
            h_re = x_re + q_re * s_re - q_im * s_im + p_re * c_re - p_im * c_im
            h_im = x_im + q_re * s_im + q_im * s_re + p_re * c_im + p_im * c_re
            xre_ref[pl.ds(r0, SUBLANES), sl] = h_re
            xim_ref[pl.ds(r0, SUBLANES), sl] = h_im
            return (jnp.where(upper, h_re, pltpu.roll(h_re, half, 0)),
                    jnp.where(upper, h_im, pltpu.roll(h_im, half, 0)))

        c_re, c_im = lax.fori_loop(0, rows // SUBLANES, body, (cre_ref[:, sl], cim_ref[:, sl]), unroll=2)
        cre_ref[:, sl] = c_re
        cim_ref[:, sl] = c_im

    y = _ssm_output(u, xre_ref, xim_ref, cw_ref, d_ref, gw_ref, gb_ref)
    for c in range(n_slab):
        utb_ref[c] = y[:, c * LANES:(c + 1) * LANES]
    for b in range(n_seq):
        for c in range(n_slab):
            out_ref[b, :, c * LANES:(c + 1) * LANES] = utb_ref[c, pl.ds(b, t_chunk, stride=n_seq), :].astype(BF)
    hre_out[...] = cre_ref[...]
    him_out[...] = cim_ref[...]


def _ssm_prompt(su, bw, cw, pq, d, gw, gb, *, t_chunk):
    n_seq, seq, w = su.shape
    ns = pq.shape[2]
    rows = n_seq * t_chunk
    blk_map = lambda i: (0, i, 0)
    const2 = lambda i: (0, 0)
    return pl.pallas_call(
        functools.partial(_ssm_prompt_kernel, lane_grp=min(512, ns)),
        grid=(seq // t_chunk,),
        in_specs=[pl.BlockSpec((n_seq, t_chunk, w), blk_map),
                  pl.BlockSpec(bw.shape, lambda i: (0, 0, 0)),
                  pl.BlockSpec(cw.shape, lambda i: (0, 0, 0, 0)),
                  pl.BlockSpec(pq.shape, lambda i: (0, 0, 0)),
                  pl.BlockSpec((1, w), const2),
                  pl.BlockSpec(gw.shape, const2),
                  pl.BlockSpec((1, w), const2)],
        out_specs=(pl.BlockSpec((n_seq, t_chunk, w), blk_map),
                   pl.BlockSpec((SUBLANES, ns), const2), pl.BlockSpec((SUBLANES, ns), const2)),
        out_shape=(jax.ShapeDtypeStruct((n_seq, seq, w), BF),
                   jax.ShapeDtypeStruct((SUBLANES, ns), F32), jax.ShapeDtypeStruct((SUBLANES, ns), F32)),
        scratch_shapes=[pltpu.VMEM((w // LANES, rows, LANES), F32),
                        pltpu.VMEM((rows, ns), F32), pltpu.VMEM((rows, ns), F32),
                        pltpu.VMEM((SUBLANES, ns), F32), pltpu.VMEM((SUBLANES, ns), F32)],
        compiler_params=_params("arbitrary"),
        name="ssm_prompt",
    )(su, bw, cw, pq, d, gw, gb)


def _ssm_sample_kernel(u_ref, h0re_ref, h0im_ref, bw_ref, cw_ref, a_ref, d_ref, gw_ref, gb_ref,
                       out_ref, hre_out, him_out, xre_ref, xim_ref, *, n_steps):
    ub = u_ref[...]
    _ssm_input_proj(ub, bw_ref, xre_ref, xim_ref)
    nb = h0re_ref.shape[0]
    a_re, a_im = a_ref[0], a_ref[1]
    h_re, h_im = h0re_ref[...], h0im_ref[...]
    for t in range(n_steps):
        rs = slice(t * nb, (t + 1) * nb)
        h_re, h_im = (a_re * h_re - a_im * h_im + xre_ref[rs, :],
                      a_re * h_im + a_im * h_re + xim_ref[rs, :])
        xre_ref[rs, :] = h_re
        xim_ref[rs, :] = h_im
    out_ref[...] = _ssm_output(ub.astype(F32), xre_ref, xim_ref, cw_ref, d_ref, gw_ref, gb_ref).astype(BF)
    hre_out[...] = h_re
    him_out[...] = h_im


def _ssm_sample(su, h0_re, h0_im, bw, cw, a, d, gw, gb, *, n_steps):
    rows, w = su.shape
    nb, ns = h0_re.shape
    full = lambda arr: pl.BlockSpec(arr.shape, lambda i, n=arr.ndim: (0,) * n)
    return pl.pallas_call(
        functools.partial(_ssm_sample_kernel, n_steps=n_steps),
        grid=(1,),
        in_specs=[full(su), full(h0_re), full(h0_im), full(bw), full(cw), full(a), full(d), full(gw), full(gb)],
        out_specs=(pl.BlockSpec((rows, w), lambda i: (0, 0)),
                   pl.BlockSpec((nb, ns), lambda i: (0, 0)), pl.BlockSpec((nb, ns), lambda i: (0, 0))),
        out_shape=(jax.ShapeDtypeStruct((rows, w), BF),
                   jax.ShapeDtypeStruct((nb, ns), F32), jax.ShapeDtypeStruct((nb, ns), F32)),
        scratch_shapes=[pltpu.VMEM((rows, ns), F32), pltpu.VMEM((rows, ns), F32)],
        compiler_params=_params("arbitrary"),
        name="ssm_sample",
    )(su, h0_re, h0_im, bw, cw, a, d, gw, gb)


def _merge_kernel(at_ref, gm_ref, ss_ref, gates_ref, x_ref, wb_ref, wo_ref, lg_ref, lb_ref, o_ref, *, alpha):
    d = x_ref.shape[1]
    merged = None
    for n, br in enumerate((at_ref, gm_ref, ss_ref)):
        proj = jnp.dot(br[...], wb_ref[n], preferred_element_type=F32)
        gated = gates_ref[:, n * d:(n + 1) * d].astype(F32) * proj
        merged = gated if merged is None else merged + gated
    mix = jnp.dot(merged.astype(BF), wo_ref[...], preferred_element_type=F32)
    o_ref[...] = _layer_norm(alpha * x_ref[...] + mix, lg_ref[...], lb_ref[...])


def _merge(attn, gm, ssm, gates, x, wb_b, wo_b, ln_g, ln_b, *, tm, alpha):
    R, d = x.shape
    w = attn.shape[1]
    row_map = lambda i: (i, 0)
    const2 = lambda i: (0, 0)
    return pl.pallas_call(
        functools.partial(_merge_kernel, alpha=alpha),
        grid=(R // tm,),
        in_specs=[pl.BlockSpec((tm, w), row_map), pl.BlockSpec((tm, w), row_map),
                  pl.BlockSpec((tm, w), row_map),
                  pl.BlockSpec((tm, gates.shape[1]), row_map),
                  pl.BlockSpec((tm, d), row_map),
                  pl.BlockSpec(wb_b.shape, lambda i: (0, 0, 0)),
                  pl.BlockSpec(wo_b.shape, const2),
                  pl.BlockSpec((1, d), const2), pl.BlockSpec((1, d), const2)],
        out_specs=pl.BlockSpec((tm, d), row_map),
        out_shape=jax.ShapeDtypeStruct((R, d), F32),
        compiler_params=_params("parallel"),
        name="merge",
    )(attn, gm, ssm, gates, x, wb_b, wo_b, ln_g, ln_b)


def _ffn_kernel(x_ref, wi_ref, wo_ref, lg_ref, lb_ref, o_ref, hid_ref, *, alpha, tf):
    x = x_ref[...]
    xb = x.astype(BF)
    d_ff = wo_ref.shape[0]
    for c in range(d_ff // tf):
        gate = jnp.dot(xb, wi_ref[:, c * tf:(c + 1) * tf], preferred_element_type=F32)
        up = jnp.dot(xb, wi_ref[:, d_ff + c * tf:d_ff + (c + 1) * tf], preferred_element_type=F32)
        hid_ref[:, c * tf:(c + 1) * tf] = (gate * jax.nn.sigmoid(gate) * up).astype(BF)
    f = jnp.dot(hid_ref[...], wo_ref[...], preferred_element_type=F32)
    o_ref[...] = _layer_norm(alpha * x + f, lg_ref[...], lb_ref[...])


def _ffn(x, w_in_b, w_out_b, ln_g, ln_b, *, tm, tf, alpha):
    R, d = x.shape
    d_ff = w_out_b.shape[0]
    row_map = lambda i: (i, 0)
    const2 = lambda i: (0, 0)
    return pl.pallas_call(
        functools.partial(_ffn_kernel, alpha=alpha, tf=tf),
        grid=(R // tm,),
        in_specs=[pl.BlockSpec((tm, d), row_map),
                  pl.BlockSpec(w_in_b.shape, const2, pipeline_mode=pl.Buffered(1)),
                  pl.BlockSpec(w_out_b.shape, const2, pipeline_mode=pl.Buffered(1)),
                  pl.BlockSpec((1, d), const2), pl.BlockSpec((1, d), const2)],
        out_specs=pl.BlockSpec((tm, d), row_map),
        out_shape=jax.ShapeDtypeStruct((R, d), F32),
        scratch_shapes=[pltpu.VMEM((tm, d_ff), BF)],
        compiler_params=_params("parallel"),
        name="ffn",
    )(x, w_in_b, w_out_b, ln_g, ln_b)


def _rope_tables(pos, hd):
    half = hd // 2
    inv = ROPE_THETA ** (-jnp.arange(half, dtype=F32) / half)
    ang = pos.astype(F32)[:, None] * inv[None, :]
    cos, sin = jnp.cos(ang), jnp.sin(ang)
    reps = LANES // hd
    return (jnp.tile(jnp.concatenate([cos, cos], axis=1), (1, reps)),
            jnp.tile(jnp.concatenate([-sin, sin], axis=1), (1, reps)))


def _ssm_discretize(a_re, a_im, log_dt, b_re, b_im):
    dt = jnp.exp(log_dt.astype(F32))[:, None]
    ar, ai = a_re.astype(F32), a_im.astype(F32)
    mag = jnp.exp(dt * ar)
    abar_re = mag * jnp.cos(dt * ai)
    abar_im = mag * jnp.sin(dt * ai)
    nr, ni = abar_re - 1.0, abar_im
    den = ar * ar + ai * ai
    c_re = (nr * ar + ni * ai) / den
    c_im = (ni * ar - nr * ai) / den
    br, bi = b_re.astype(F32), b_im.astype(F32)
    bbar_re = c_re[..., None] * br - c_im[..., None] * bi
    bbar_im = c_re[..., None] * bi + c_im[..., None] * br
    return abar_re, abar_im, bbar_re, bbar_im


def _ssm_weights(a_re, a_im, log_dt, b_re, b_im, c_re, c_im):
    n_grp, n_state, grp_ch = b_re.shape
    abar_re, abar_im, bbar_re, bbar_im = _ssm_discretize(a_re, a_im, log_dt, b_re, b_im)
    gps = MXU_K // grp_ch
    n_slab = n_grp // gps
    eye = jnp.eye(gps, dtype=F32)

    def in_blocks(bb):
        bb = bb.reshape(n_slab, gps, n_state, grp_ch)
        return jnp.einsum('sgpc,gk->sgckp', bb, eye).reshape(n_slab, gps * grp_ch, gps * n_state)

    def out_blocks(cc):
        cc = cc.reshape(n_slab, gps, grp_ch, n_state)
        return jnp.einsum('sgcp,gk->sgpkc', cc, eye).reshape(n_slab, gps * n_state, gps * grp_ch)

    bw = jnp.concatenate([in_blocks(bbar_re), in_blocks(bbar_im)], axis=2).astype(BF)
    cw = jnp.stack([out_blocks(c_re.astype(F32)), -out_blocks(c_im.astype(F32))]).astype(BF)
    a_flat = jnp.stack([abar_re.reshape(1, -1), abar_im.reshape(1, -1)])
    a2_re = abar_re * abar_re - abar_im * abar_im
    a2_im = 2.0 * abar_re * abar_im
    half = SUBLANES // 2

    def halves(lo, hi):
        return jnp.concatenate([jnp.broadcast_to(lo.reshape(1, -1), (half, lo.size)),
                                jnp.broadcast_to(hi.reshape(1, -1), (half, hi.size))], axis=0)

    zero = jnp.zeros_like(abar_re)
    pq = jnp.stack([halves(abar_re, a2_re), halves(abar_im, a2_im),
                    halves(zero, abar_re), halves(zero, abar_im)])
    return bw, cw, a_flat, pq


def _lambda_init(l):
    return 0.8 - 0.6 * math.exp(-0.3 * l)


def kernel(x_prompt, x_sample, cache_k, cache_v, page_table, state_ssm_re, state_ssm_im, w_in, b_gate, lambda_q1, lambda_k1, lambda_q2, lambda_k2, attn_norm_g, gmlp_ln_g, gmlp_ln_b, gmlp_ws, gmlp_bs, ssm_a_re, ssm_a_im, ssm_log_dt, ssm_b_re, ssm_b_im, ssm_c_re, ssm_c_im, ssm_d, ssm_glu_w, ssm_glu_b, w_branch, w_out, ln1_g, ln1_b, ln2_g, ln2_b, w_ffn_in, w_ffn_out):
    bp, lp, d_model = x_prompt.shape
    bs, ls, _ = x_sample.shape
    depth = w_in.shape[0]
    n_pool, page, n_heads, dv = cache_k.shape[1:]
    hd = dv // 2
    n_pages = page_table.shape[1]
    n_past = n_pages * page
    wbr = ssm_glu_w.shape[1]
    n_grp_g, chunk = gmlp_ws.shape[1], gmlp_ws.shape[2]
    n_grp_s, n_state, grp_ch = ssm_b_re.shape[1:]
    ns = n_grp_s * n_state
    d_ff = w_ffn_out.shape[1]
    alpha = (2 * depth) ** 0.25
    rp, rs = bp * lp, bs * ls
    assert bp == SUBLANES // 2 and n_heads * dv == wbr and wbr % MXU_K == 0 and LANES % hd == 0
    assert dv == LANES and ls <= SUBLANES and rs == chunk and lp % page == 0

    xp = x_prompt.reshape(rp, d_model)
    xs = x_sample.transpose(1, 0, 2).reshape(rs, d_model)
    cos_p, sin_p = _rope_tables(jnp.arange(lp), hd)
    cos_s, sin_s = _rope_tables(jnp.repeat(n_past + jnp.arange(ls), bs), hd)
    ck = cache_k.reshape(depth, n_pool, page * n_heads, dv)
    cv = cache_v.reshape(depth, n_pool, page * n_heads, dv)
    pt_flat = page_table.reshape(-1).astype(jnp.int32)

    tm_p = min(512, lp)
    tk = min(512, lp)
    tf = 256 if d_ff % 256 == 0 else d_ff
    t_chunk = min(256, lp)
    gc = wbr // n_grp_g

    outs = {k: [] for k in ("ks", "vs", "hrp", "hip", "hrs", "his", "gvs")}
    kv_stacked = ()
    for l in range(depth):
        lam_init = _lambda_init(l)
        post_scale = 1.0 - lam_init
        lam = (jnp.exp(jnp.sum(lambda_q1[l].astype(F32) * lambda_k1[l].astype(F32)))
               - jnp.exp(jnp.sum(lambda_q2[l].astype(F32) * lambda_k2[l].astype(F32)))
               + lam_init).reshape(1)
        w_in_b = w_in[l].astype(BF)
        wb_b = w_branch[l].astype(BF)
        wo_b = w_out[l].astype(BF)
        wfi_b = w_ffn_in[l].astype(BF)
        wfo_b = w_ffn_out[l].astype(BF)
        glu_b16 = ssm_glu_w[l].astype(BF)
        an_g = attn_norm_g[l].reshape(1, dv)
        ws_tril = jnp.tril(gmlp_ws[l])
        bw, cw, a_flat, pq = _ssm_weights(ssm_a_re[l], ssm_a_im[l], ssm_log_dt[l], ssm_b_re[l], ssm_b_im[l],
                                          ssm_c_re[l], ssm_c_im[l])
        d_row = ssm_d[l].reshape(1, wbr)
        glu_bias = ssm_glu_b[l].reshape(1, wbr)
        ln_gg, ln_gb = gmlp_ln_g[l].reshape(1, wbr), gmlp_ln_b[l].reshape(1, wbr)
        bgate = b_gate[l].reshape(1, -1)
        l1g, l1b = ln1_g[l].reshape(1, d_model), ln1_b[l].reshape(1, d_model)
        l2g, l2b = ln2_g[l].reshape(1, d_model), ln2_b[l].reshape(1, d_model)

        bias_p = jnp.repeat(gmlp_bs[l].T, gc, axis=1)
        q, kt, kf_all, vx, vf_all, gm, su, gates = _in_proj(
            xp, w_in_b, cos_p, sin_p, bgate, ln_gg, ln_gb, ws_tril.astype(BF), bias_p, kv_stacked,
            wb=wbr, hd=hd, n_heads=n_heads, tm=tm_p, tk=tk, rows_per_seq=lp, prompt=True,
            layer=l, depth=depth)
        kv_stacked = (kf_all, vf_all)
        attn = _attn_prompt(lam, q, kt, vx, an_g, n_seq=bp, seq=lp, n_heads=n_heads, hd=hd, tk=tk,
                            post_scale=post_scale)
        ssm, hre, him = _ssm_prompt(su.reshape(bp, lp, wbr), bw, cw, pq, d_row, glu_b16, glu_bias,
                                    t_chunk=t_chunk)
        x1 = _merge(attn, gm, ssm.reshape(rp, wbr), gates, xp, wb_b, wo_b, l1g, l1b,
                    tm=min(512, rp), alpha=alpha)
        xp = _ffn(x1, wfi_b, wfo_b, l2g, l2b, tm=min(512, rp), tf=tf, alpha=alpha)
        outs["hrp"].append(hre[SUBLANES // 2:].reshape(bp, n_grp_s, n_state))
        outs["hip"].append(him[SUBLANES // 2:].reshape(bp, n_grp_s, n_state))

        ws_s = jnp.einsum('gts,bc->gtbsc', ws_tril[:, :ls, :ls], jnp.eye(bs, dtype=F32)).reshape(n_grp_g, rs, rs)
        bias_s = jnp.repeat(jnp.repeat(gmlp_bs[l][:, :ls].T, bs, axis=0), gc, axis=1)
        q, kb, kf, vb, vf, gm, vn, su, gates = _in_proj(
            xs, w_in_b, cos_s, sin_s, bgate, ln_gg, ln_gb, ws_s.astype(BF), bias_s,
            wb=wbr, hd=hd, n_heads=n_heads, tm=rs, tk=rs, rows_per_seq=rs, prompt=False)
        q3 = q.reshape(ls, bs, wbr).transpose(1, 0, 2)
        zq = jnp.zeros((bs, SUBLANES - ls, wbr), BF)
        q16 = jnp.concatenate([q3, zq, q3, zq], axis=1).reshape(bs * 2 * SUBLANES, wbr)
        pad_new = lambda a: jnp.pad(a.reshape(ls, bs, wbr).transpose(1, 0, 2),
                                    ((0, 0), (0, page - ls), (0, 0))).reshape(bs * page, wbr)
        attn8 = _attn_sample(pt_flat, lam, q16, pad_new(kb), pad_new(vb), an_g, ck, cv,
                             layer=l, n_seq=bs, n_pages=n_pages, n_heads=n_heads, hd=hd,
                             post_scale=post_scale)
        attn = attn8.reshape(bs, SUBLANES, wbr)[:, :ls].transpose(1, 0, 2).reshape(rs, wbr).astype(BF)
        ssm, hre, him = _ssm_sample(su, state_ssm_re[l].reshape(bs, ns), state_ssm_im[l].reshape(bs, ns),
                                    bw, cw, a_flat, d_row, glu_b16, glu_bias, n_steps=ls)
        x1 = _merge(attn, gm, ssm, gates, xs, wb_b, wo_b, l1g, l1b, tm=rs, alpha=alpha)
        xs = _ffn(x1, wfi_b, wfo_b, l2g, l2b, tm=rs, tf=tf, alpha=alpha)
        to_seq_major = lambda a: a.reshape(ls, bs, -1).transpose(1, 0, 2)
        outs["ks"].append(to_seq_major(kf).reshape(bs, ls, n_heads, dv))
        outs["vs"].append(to_seq_major(vf).reshape(bs, ls, n_heads, dv))
        outs["hrs"].append(hre.reshape(bs, n_grp_s, n_state))
        outs["his"].append(him.reshape(bs, n_grp_s, n_state))
        outs["gvs"].append(to_seq_major(vn))

    st = lambda k: jnp.stack(outs[k])
    kv_shape = (depth, bp, lp // page, page, n_heads, dv)
    return (xp.reshape(bp, lp, d_model), xs.reshape(ls, bs, d_model).transpose(1, 0, 2),
            kv_stacked[0].reshape(kv_shape), kv_stacked[1].reshape(kv_shape), st("ks"), st("vs"),
            st("hrp"), st("hip"), st("hrs"), st("his"), st("gvs"))
```

```python
import functools
import math

import jax
import jax.numpy as jnp
from jax import lax
from jax.experimental import pallas as pl
from jax.experimental.pallas import tpu as pltpu

BF = jnp.bfloat16
F32 = jnp.float32

LN_EPS = 1e-5
ROPE_THETA = 10000.0
NEG_BIG = -1e30
LOG2E = math.log2(math.e)
LANES = 128
SUBLANES = 8
MXU_K = 256
VMEM_LIMIT = 56 * 1024 * 1024


def _params(*sem):
    return pltpu.CompilerParams(dimension_semantics=sem, vmem_limit_bytes=VMEM_LIMIT)


def _layer_norm(r, g, b):
    mu = jnp.mean(r, axis=-1, keepdims=True)
    d = r - mu
    var = jnp.mean(d * d, axis=-1, keepdims=True)
    return d * lax.rsqrt(var + LN_EPS) * g + b


def _in_proj_kernel(*refs, hd, n_heads, q_scale, prompt, tk, n_alias):
    x_ref, w_ref, cos_ref, sin_ref, bg_ref, lng_ref, lnb_ref, ws_ref, bias_ref = refs[:9]
    rest = refs[9 + n_alias:]
    if prompt:
        q_ref, kt_ref, kf_ref, vx_ref, vf_ref, gm_ref, su_ref, gates_ref = rest
    else:
        q_ref, kb_ref, kf_ref, vb_ref, vf_ref, gm_ref, vn_ref, su_ref, gates_ref = rest
    tm = x_ref.shape[0]
    wb = q_ref.shape[1]
    dv = 2 * hd
    half = hd // 2
    xb = x_ref[...].astype(BF)

    def col(c):
        return jnp.dot(xb, w_ref[:, c * wb:(c + 1) * wb], preferred_element_type=F32)

    def rope(a):
        cos = cos_ref[...]
        sin = sin_ref[...]
        outs = []
        for c in range(a.shape[1] // LANES):
            blk = a[:, c * LANES:(c + 1) * LANES]
            lane = lax.broadcasted_iota(jnp.int32, blk.shape, 1)
            swapped = jnp.where((lane % hd) < half,
                                pltpu.roll(blk, LANES - half, 1), pltpu.roll(blk, half, 1))
            outs.append(blk * cos + swapped * sin)
        return jnp.concatenate(outs, axis=1)

    def store_head_rows(dst_ref, a):
        for h in range(n_heads):
            dst_ref[pl.ds(h, tm, stride=n_heads), :] = a[:, h * dv:(h + 1) * dv]

    q_ref[...] = (rope(col(0)) * q_scale).astype(BF)

    k = rope(col(1))
    if prompt:
        store_head_rows(kf_ref, k)
        for t in range(tm // tk):
            kt_ref[t] = k[t * tk:(t + 1) * tk, :].T.astype(BF)
    else:
        kf_ref[...] = k
        kb_ref[...] = k.astype(BF)

    v = col(2)
    if prompt:
        store_head_rows(vf_ref, v)
        ones = jnp.ones((tm, dv), BF)
        for h in range(n_heads):
            vx_ref[:, 2 * h * dv:(2 * h + 1) * dv] = v[:, h * dv:(h + 1) * dv].astype(BF)
            vx_ref[:, (2 * h + 1) * dv:(2 * h + 2) * dv] = ones
    else:
        vf_ref[...] = v
        vb_ref[...] = v.astype(BF)

    vn = _layer_norm(jax.nn.gelu(col(4)), lng_ref[...], lnb_ref[...])
    if not prompt:
        vn_ref[...] = vn
    vb16 = vn.astype(BF)
    u = jax.nn.gelu(col(3))
    n_grp, chunk = ws_ref.shape[0], ws_ref.shape[1]
    gc = wb // n_grp
    for n in range(tm // chunk):
        rs = slice(n * chunk, (n + 1) * chunk)
        for g in range(n_grp):
            cs = slice(g * gc, (g + 1) * gc)
            mixed = jnp.dot(ws_ref[g], vb16[rs, cs], preferred_element_type=F32) + bias_ref[:, cs]
            gm_ref[rs, cs] = (u[rs, cs] * mixed).astype(BF)

    su_ref[...] = col(5).astype(BF)

    for n in range(gates_ref.shape[1] // wb):
        cs = slice(n * wb, (n + 1) * wb)
        gates_ref[:, cs] = jax.nn.sigmoid(col(6 + n) + bg_ref[:, cs]).astype(BF)


def _in_proj(x, w_b, cos_t, sin_t, b_gate, ln_g, ln_b, ws_b, bias_full, kv_prev=(), *,
             wb, hd, n_heads, tm, tk, rows_per_seq, prompt, layer=0, depth=1):
    R, D = x.shape
    n_gate = w_b.shape[1] // wb - 6
    nt = rows_per_seq // tm
    dv = 2 * hd
    chunk = ws_b.shape[1]
    row_map = lambda i: (i, 0)
    const2 = lambda i: (0, 0)
    tab_map = lambda i: (i % nt, 0)
    blk = lambda n=1: pl.BlockSpec((tm, n * wb), row_map)
    sds = jax.ShapeDtypeStruct
    if prompt:
        stacked = sds((depth, R * n_heads, dv), F32)
        stacked_spec = pl.BlockSpec((None, tm * n_heads, dv), lambda i: (layer, i, 0))
        out_shape = (sds((R, wb), BF), sds((R // tk, wb, tk), BF), stacked, sds((R, 2 * wb), BF), stacked,
                     sds((R, wb), BF), sds((R, wb), BF), sds((R, n_gate * wb), BF))
        out_specs = (blk(), pl.BlockSpec((tm // tk, wb, tk), lambda i: (i, 0, 0)), stacked_spec,
                     blk(2), stacked_spec, blk(), blk(), blk(n_gate))
        aliases = {9 + n: (2, 4)[n] for n in range(len(kv_prev))}
    else:
        out_shape = (sds((R, wb), BF), sds((R, wb), BF), sds((R, wb), F32), sds((R, wb), BF), sds((R, wb), F32),
                     sds((R, wb), BF), sds((R, wb), F32), sds((R, wb), BF), sds((R, n_gate * wb), BF))
        out_specs = (blk(),) * 8 + (blk(n_gate),)
        aliases = {}
    return pl.pallas_call(
        functools.partial(_in_proj_kernel, hd=hd, n_heads=n_heads, q_scale=hd ** -0.5 * LOG2E,
                          prompt=prompt, tk=tk, n_alias=len(kv_prev)),
        grid=(R // tm,),
        in_specs=[pl.BlockSpec((tm, D), row_map),
                  pl.BlockSpec(w_b.shape, const2, pipeline_mode=pl.Buffered(1)),
                  pl.BlockSpec((tm, LANES), tab_map),
                  pl.BlockSpec((tm, LANES), tab_map),
                  pl.BlockSpec((1, n_gate * wb), const2),
                  pl.BlockSpec((1, wb), const2), pl.BlockSpec((1, wb), const2),
                  pl.BlockSpec(ws_b.shape, lambda i: (0, 0, 0)),
                  pl.BlockSpec((chunk, wb), const2)]
                 + [pl.BlockSpec(memory_space=pl.ANY)] * len(kv_prev),
        out_specs=out_specs,
        out_shape=out_shape,
        input_output_aliases=aliases,
        compiler_params=_params("parallel"),
        name="in_proj_prompt" if prompt else "in_proj_sample",
    )(x, w_b, cos_t, sin_t, b_gate, ln_g, ln_b, ws_b, bias_full, *kv_prev)


def _attn_prompt_kernel(lam_ref, q_ref, kt_ref, vx_ref, g_ref, o_ref, qs_ref, m_ref, acc_ref,
                        *, tq, tk, hd, n_heads, post_scale, unroll):
    qi = pl.program_id(1)
    dv = 2 * hd
    for h in range(n_heads):
        q = q_ref[:, h * dv:(h + 1) * dv].astype(F32)
        lane = lax.broadcasted_iota(jnp.int32, q.shape, 1)
        qs_ref[h, 0:tq, :] = jnp.where(lane < hd, q, 0.0).astype(BF)
        qs_ref[h, tq:2 * tq, :] = jnp.where(lane >= hd, q, 0.0).astype(BF)
    m_ref[...] = jnp.full(m_ref.shape, -jnp.inf, F32)
    acc_ref[...] = jnp.zeros(acc_ref.shape, F32)

    def tile(ki, masked):
        k0 = pl.multiple_of(ki * tk, tk)
        for h in range(n_heads):
            s = jnp.dot(qs_ref[h], kt_ref[ki, h * dv:(h + 1) * dv, :], preferred_element_type=F32)
            if masked:
                row = lax.broadcasted_iota(jnp.int32, s.shape, 0)
                col = lax.broadcasted_iota(jnp.int32, s.shape, 1)
                s = jnp.where(ki * tk + col <= qi * tq + row % tq, s, NEG_BIG)
            m_prev = m_ref[h]
            m_new = jnp.maximum(m_prev, jnp.max(s, axis=1, keepdims=True))
            alpha = jnp.exp2(m_prev - m_new)
            p = jnp.exp2(s - m_new[:, :1]).astype(BF)
            pv = jnp.dot(p, vx_ref[pl.ds(k0, tk), 2 * h * dv:(2 * h + 2) * dv], preferred_element_type=F32)
            acc_ref[h, :, 0:dv] = alpha * acc_ref[h, :, 0:dv] + pv[:, 0:dv]
            acc_ref[h, :, dv:2 * dv] = alpha * acc_ref[h, :, dv:2 * dv] + pv[:, dv:2 * dv]
            m_ref[h] = m_new

    n_full = (qi * tq) // tk
    n_groups = n_full // unroll
    tail = n_full + 1 - n_groups * unroll

    def full_group(kg, carry):
        for u in range(unroll):
            tile(unroll * kg + u, False)
        return carry

    lax.fori_loop(0, n_groups, full_group, 0)
    for r in range(1, unroll + 1):
        @pl.when(tail == r)
        def _(r=r):
            for u in range(r - 1):
                tile(n_groups * unroll + u, False)
            tile(n_groups * unroll + r - 1, True)

    for h in range(n_heads):
        t = acc_ref[h, :, 0:dv] / acc_ref[h, :, dv:2 * dv]
        o = t[0:tq] - lam_ref[0] * t[tq:2 * tq]
        ms = jnp.mean(o * o, axis=1, keepdims=True)
        o_ref[:, h * dv:(h + 1) * dv] = (o * lax.rsqrt(ms + LN_EPS) * g_ref[...] * post_scale).astype(BF)


def _attn_prompt(lam, q, kt, vx, g, *, n_seq, seq, n_heads, hd, tk, post_scale):
    tq = min(256, seq)
    dv = 2 * hd
    w = n_heads * dv
    nq = seq // tq
    return pl.pallas_call(
        functools.partial(_attn_prompt_kernel, tq=tq, tk=tk, hd=hd, n_heads=n_heads, post_scale=post_scale,
                          unroll=3),
        grid=(n_seq, nq),
        in_specs=[pl.BlockSpec(memory_space=pltpu.SMEM),
                  pl.BlockSpec((tq, w), lambda b, i: (b * nq + i, 0)),
                  pl.BlockSpec((seq // tk, w, tk), lambda b, i: (b, 0, 0)),
                  pl.BlockSpec((seq, 2 * w), lambda b, i: (b, 0)),
                  pl.BlockSpec((1, dv), lambda b, i: (0, 0))],
        out_specs=pl.BlockSpec((tq, w), lambda b, i: (b * nq + i, 0)),
        out_shape=jax.ShapeDtypeStruct(q.shape, BF),
        scratch_shapes=[pltpu.VMEM((n_heads, 2 * tq, dv), BF), pltpu.VMEM((n_heads, 2 * tq, dv), F32),
                        pltpu.VMEM((n_heads, 2 * tq, 2 * dv), F32)],
        compiler_params=_params("parallel", "arbitrary"),
        name="attn_prompt",
    )(lam, q, kt, vx, g)


def _attn_sample_kernel(pt_ref, lam_ref, q_ref, kn_ref, vn_ref, g_ref, *rest,
                        n_pg, n_heads, hd, post_scale):
    k_refs = rest[:n_pg]
    v_refs = rest[n_pg:2 * n_pg]
    o_ref, qs_ref, m_ref, l_ref, acc_ref = rest[2 * n_pg:]
    j = pl.program_id(1)
    dv = 2 * hd
    nr = qs_ref.shape[1]
    page = k_refs[0].shape[0] // n_heads
    nt_dims = (((1,), (1,)), ((), ()))

    @pl.when(j == 0)
    def _():
        for h in range(n_heads):
            qh = q_ref[:, h * dv:(h + 1) * dv].astype(F32)
            row = lax.broadcasted_iota(jnp.int32, qh.shape, 0)
            lane = lax.broadcasted_iota(jnp.int32, qh.shape, 1)
            keep = (row < nr // 2) == (lane < hd)
            qs_ref[h] = jnp.where(keep, qh, 0.0).astype(BF)
        m_ref[...] = jnp.full(m_ref.shape, -jnp.inf, F32)
        l_ref[...] = jnp.zeros(l_ref.shape, F32)
        acc_ref[...] = jnp.zeros(acc_ref.shape, F32)

    def update(h, s, pv_fn):
        m_prev = m_ref[h]
        m_new = jnp.maximum(m_prev, jnp.max(s, axis=1, keepdims=True))
        alpha = jnp.exp2(m_prev - m_new)
        p = jnp.exp2(s - m_new[:, :1])
        l_ref[h] = alpha * l_ref[h] + jnp.sum(p, axis=1, keepdims=True)
        acc_ref[h] = alpha * acc_ref[h] + pv_fn(p.astype(BF))
        m_ref[h] = m_new

    def head_rows(page_refs, h):
        return jnp.concatenate([r[pl.ds(h, page, stride=n_heads), :] for r in page_refs], axis=0).astype(BF)

    for h in range(n_heads):
        s = lax.dot_general(qs_ref[h], head_rows(k_refs, h), nt_dims, preferred_element_type=F32)
        update(h, s, lambda p, h=h: jnp.dot(p, head_rows(v_refs, h), preferred_element_type=F32))

    @pl.when(j == pl.num_programs(1) - 1)
    def _():
        for h in range(n_heads):
            hs = slice(h * dv, (h + 1) * dv)
            s = lax.dot_general(qs_ref[h], kn_ref[:, hs], nt_dims, preferred_element_type=F32)
            row = lax.broadcasted_iota(jnp.int32, s.shape, 0)
            col = lax.broadcasted_iota(jnp.int32, s.shape, 1)
            s = jnp.where(col <= row % (nr // 2), s, NEG_BIG)
            update(h, s, lambda p, hs=hs: jnp.dot(p, vn_ref[:, hs], preferred_element_type=F32))
            t = acc_ref[h] / l_ref[h]
            o = t[0:nr // 2] - lam_ref[0] * t[nr // 2:nr]
            ms = jnp.mean(o * o, axis=1, keepdims=True)
            o_ref[:, hs] = o * lax.rsqrt(ms + LN_EPS) * g_ref[...] * post_scale


def _attn_sample(page_table_flat, lam, q16, kn, vn, g, cache_k, cache_v, *,
                 layer, n_seq, n_pages, n_heads, hd, post_scale):
    n_pg = math.gcd(16, n_pages)
    nr = 2 * SUBLANES
    dv = 2 * hd
    w = n_heads * dv
    page_rows = cache_k.shape[2]
    page = page_rows // n_heads

    def kv_spec(g):
        return pl.BlockSpec((None, None, page_rows, dv),
                            lambda b, j, pt: (layer, pt[b * n_pages + j * n_pg + g], 0, 0))

    grid_spec = pltpu.PrefetchScalarGridSpec(
        num_scalar_prefetch=1,
        grid=(n_seq, n_pages // n_pg),
        in_specs=[pl.BlockSpec(memory_space=pltpu.SMEM),
                  pl.BlockSpec((nr, w), lambda b, j, pt: (b, 0)),
                  pl.BlockSpec((page, w), lambda b, j, pt: (b, 0)),
                  pl.BlockSpec((page, w), lambda b, j, pt: (b, 0)),
                  pl.BlockSpec((1, dv), lambda b, j, pt: (0, 0))]
                 + [kv_spec(g) for g in range(n_pg)] + [kv_spec(g) for g in range(n_pg)],
        out_specs=pl.BlockSpec((nr // 2, w), lambda b, j, pt: (b, 0)),
        scratch_shapes=[pltpu.VMEM((n_heads, nr, dv), BF), pltpu.VMEM((n_heads, nr, dv), F32),
                        pltpu.VMEM((n_heads, nr, dv), F32), pltpu.VMEM((n_heads, nr, dv), F32)])
    return pl.pallas_call(
        functools.partial(_attn_sample_kernel, n_pg=n_pg, n_heads=n_heads, hd=hd, post_scale=post_scale),
        grid_spec=grid_spec,
        out_shape=jax.ShapeDtypeStruct((n_seq * (nr // 2), w), F32),
        compiler_params=_params("parallel", "arbitrary"),
        name="attn_sample",
    )(page_table_flat, lam, q16, kn, vn, g, *([cache_k] * n_pg), *([cache_v] * n_pg))


def _ssm_input_proj(ub, bw_ref, xre_ref, xim_ref):
    n_slab = bw_ref.shape[0]
    ns = xre_ref.shape[1] // n_slab
    for h in range(n_slab):
        uh = ub[:, h * MXU_K:(h + 1) * MXU_K]
        xre_ref[:, h * ns:(h + 1) * ns] = jnp.dot(uh, bw_ref[h, :, 0:ns], preferred_element_type=F32)
        xim_ref[:, h * ns:(h + 1) * ns] = jnp.dot(uh, bw_ref[h, :, ns:2 * ns], preferred_element_type=F32)


def _ssm_output(u, hre_ref, him_ref, cw_ref, d_ref, gw_ref, gb_ref):
    n_slab = cw_ref.shape[1]
    ns = hre_ref.shape[1] // n_slab
    ys = []
    for h in range(n_slab):
        ss = slice(h * ns, (h + 1) * ns)
        ys.append(jnp.dot(hre_ref[:, ss].astype(BF), cw_ref[0, h], preferred_element_type=F32)
                  + jnp.dot(him_ref[:, ss].astype(BF), cw_ref[1, h], preferred_element_type=F32))
    y = jnp.concatenate(ys, axis=1) + d_ref[...] * u
    a = jax.nn.gelu(y)
    z = jnp.dot(a.astype(BF), gw_ref[...], preferred_element_type=F32) + gb_ref[...]
    return a * jax.nn.sigmoid(z)


def _ssm_prompt_kernel(u_ref, bw_ref, cw_ref, pq_ref, d_ref, gw_ref, gb_ref,
                       out_ref, hre_out, him_out, utb_ref, xre_ref, xim_ref, cre_ref, cim_ref, *, lane_grp):
    @pl.when(pl.program_id(0) == 0)
    def _():
        cre_ref[...] = jnp.zeros(cre_ref.shape, F32)
        cim_ref[...] = jnp.zeros(cim_ref.shape, F32)

    n_seq, t_chunk = u_ref.shape[0], u_ref.shape[1]
    n_slab = utb_ref.shape[0]
    for b in range(n_seq):
        ub = u_ref[b].astype(F32)
        for c in range(n_slab):
            utb_ref[c, pl.ds(b, t_chunk, stride=n_seq), :] = ub[:, c * LANES:(c + 1) * LANES]
    u = jnp.concatenate([utb_ref[c] for c in range(n_slab)], axis=1)
    ub16 = u.astype(BF)

    rows, ns = xre_ref.shape
    half = SUBLANES // 2
    ns_slab = ns // bw_ref.shape[0]
    n_lg = ns // lane_grp

    def input_proj(lg):
        slab, off = divmod(lg * lane_grp, ns_slab)
        uh = ub16[:, slab * MXU_K:(slab + 1) * MXU_K]
        sl = slice(lg * lane_grp, (lg + 1) * lane_grp)
        xre_ref[:, sl] = jnp.dot(uh, bw_ref[slab, :, off:off + lane_grp], preferred_element_type=F32)
        xim_ref[:, sl] = jnp.dot(uh, bw_ref[slab, :, ns_slab + off:ns_slab + off + lane_grp],
                                 preferred_element_type=F32)

    input_proj(0)
    for lg in range(n_lg):
        if lg + 1 < n_lg:
            input_proj(lg + 1)
        sl = slice(lg * lane_grp, (lg + 1) * lane_grp)
        p_re, p_im, q_re, q_im = pq_ref[0, :, sl], pq_ref[1, :, sl], pq_ref[2, :, sl], pq_ref[3, :, sl]
        upper = lax.broadcasted_iota(jnp.int32, (SUBLANES, lane_grp), 0) >= half
        c_re, c_im = cre_ref[:, sl], cim_ref[:, sl]
        for i in range(rows // SUBLANES):
            rs = slice(i * SUBLANES, (i + 1) * SUBLANES)
            x_re = xre_ref[rs, sl]
            x_im = xim_ref[rs, sl]
            s_re = pltpu.roll(x_re, half, 0)
            s_im = pltpu.roll(x_im, half, 0)
            h_re = x_re + q_re * s_re - q_im * s_im + p_re * c_re - p_im * c_im
            h_im = x_im + q_re * s_im + q_im * s_re + p_re * c_im + p_im * c_re
            xre_ref[rs, sl] = h_re
            xim_ref[rs, sl] = h_im
            c_re = jnp.where(upper, h_re, pltpu.roll(h_re, half, 0))
            c_im = jnp.where(upper, h_im, pltpu.roll(h_im, half, 0))
        cre_ref[:, sl] = c_re
        cim_ref[:, sl] = c_im

    y = _ssm_output(u, xre_ref, xim_ref, cw_ref, d_ref, gw_ref, gb_ref)
    for c in range(n_slab):
        utb_ref[c] = y[:, c * LANES:(c + 1) * LANES]
    for b in range(n_seq):
        for c in range(n_slab):
            out_ref[b, :, c * LANES:(c + 1) * LANES] = utb_ref[c, pl.ds(b, t_chunk, stride=n_seq), :].astype(BF)
    hre_out[...] = cre_ref[...]
    him_out[...] = cim_ref[...]


def _ssm_prompt(su, bw, cw, pq, d, gw, gb, *, t_chunk):
    n_seq, seq, w = su.shape
    ns = pq.shape[2]
    rows = n_seq * t_chunk
    blk_map = lambda i: (0, i, 0)
    const2 = lambda i: (0, 0)
    return pl.pallas_call(
        functools.partial(_ssm_prompt_kernel, lane_grp=min(512, ns)),
        grid=(seq // t_chunk,),
        in_specs=[pl.BlockSpec((n_seq, t_chunk, w), blk_map),
                  pl.BlockSpec(bw.shape, lambda i: (0, 0, 0)),
                  pl.BlockSpec(cw.shape, lambda i: (0, 0, 0, 0)),
                  pl.BlockSpec(pq.shape, lambda i: (0, 0, 0)),
                  pl.BlockSpec((1, w), const2),
                  pl.BlockSpec(gw.shape, const2),
                  pl.BlockSpec((1, w), const2)],
        out_specs=(pl.BlockSpec((n_seq, t_chunk, w), blk_map),
                   pl.BlockSpec((SUBLANES, ns), const2), pl.BlockSpec((SUBLANES, ns), const2)),
        out_shape=(jax.ShapeDtypeStruct((n_seq, seq, w), BF),
                   jax.ShapeDtypeStruct((SUBLANES, ns), F32), jax.ShapeDtypeStruct((SUBLANES, ns), F32)),
        scratch_shapes=[pltpu.VMEM((w // LANES, rows, LANES), F32),
                        pltpu.VMEM((rows, ns), F32), pltpu.VMEM((rows, ns), F32),
                        pltpu.VMEM((SUBLANES, ns), F32), pltpu.VMEM((SUBLANES, ns), F32)],
        compiler_params=_params("arbitrary"),
        name="ssm_prompt",
    )(su, bw, cw, pq, d, gw, gb)


def _ssm_sample_kernel(u_ref, h0re_ref, h0im_ref, bw_ref, cw_ref, a_ref, d_ref, gw_ref, gb_ref,
                       out_ref, hre_out, him_out, xre_ref, xim_ref, *, n_steps):
    ub = u_ref[...]
    _ssm_input_proj(ub, bw_ref, xre_ref, xim_ref)
    nb = h0re_ref.shape[0]
    a_re, a_im = a_ref[0], a_ref[1]
    h_re, h_im = h0re_ref[...], h0im_ref[...]
    for t in range(n_steps):
        rs = slice(t * nb, (t + 1) * nb)
        h_re, h_im = (a_re * h_re - a_im * h_im + xre_ref[rs, :],
                      a_re * h_im + a_im * h_re + xim_ref[rs, :])
        xre_ref[rs, :] = h_re
        xim_ref[rs, :] = h_im
    out_ref[...] = _ssm_output(ub.astype(F32), xre_ref, xim_ref, cw_ref, d_ref, gw_ref, gb_ref).astype(BF)
    hre_out[...] = h_re
    him_out[...] = h_im


def _ssm_sample(su, h0_re, h0_im, bw, cw, a, d, gw, gb, *, n_steps):
    rows, w = su.shape
    nb, ns = h0_re.shape
    full = lambda arr: pl.BlockSpec(arr.shape, lambda i, n=arr.ndim: (0,) * n)
    return pl.pallas_call(
        functools.partial(_ssm_sample_kernel, n_steps=n_steps),
        grid=(1,),
        in_specs=[full(su), full(h0_re), full(h0_im), full(bw), full(cw), full(a), full(d), full(gw), full(gb)],
        out_specs=(pl.BlockSpec((rows, w), lambda i: (0, 0)),
                   pl.BlockSpec((nb, ns), lambda i: (0, 0)), pl.BlockSpec((nb, ns), lambda i: (0, 0))),
        out_shape=(jax.ShapeDtypeStruct((rows, w), BF),
                   jax.ShapeDtypeStruct((nb, ns), F32), jax.ShapeDtypeStruct((nb, ns), F32)),
        scratch_shapes=[pltpu.VMEM((rows, ns), F32), pltpu.VMEM((rows, ns), F32)],
        compiler_params=_params("arbitrary"),
        name="ssm_sample",
    )(su, h0_re, h0_im, bw, cw, a, d, gw, gb)


def _merge_kernel(at_ref, gm_ref, ss_ref, gates_ref, x_ref, wb_ref, wo_ref, lg_ref, lb_ref, o_ref, *, alpha):
    d = x_ref.shape[1]
    merged = None
    for n, br in enumerate((at_ref, gm_ref, ss_ref)):
        proj = jnp.dot(br[...], wb_ref[n], preferred_element_type=F32)
        gated = gates_ref[:, n * d:(n + 1) * d].astype(F32) * proj
        merged = gated if merged is None else merged + gated
    mix = jnp.dot(merged.astype(BF), wo_ref[...], preferred_element_type=F32)
    o_ref[...] = _layer_norm(alpha * x_ref[...] + mix, lg_ref[...], lb_ref[...])


def _merge(attn, gm, ssm, gates, x, wb_b, wo_b, ln_g, ln_b, *, tm, alpha):
    R, d = x.shape
    w = attn.shape[1]
    row_map = lambda i: (i, 0)
    const2 = lambda i: (0, 0)
    return pl.pallas_call(
        functools.partial(_merge_kernel, alpha=alpha),
        grid=(R // tm,),
        in_specs=[pl.BlockSpec((tm, w), row_map), pl.BlockSpec((tm, w), row_map),
                  pl.BlockSpec((tm, w), row_map),
                  pl.BlockSpec((tm, gates.shape[1]), row_map),
                  pl.BlockSpec((tm, d), row_map),
                  pl.BlockSpec(wb_b.shape, lambda i: (0, 0, 0)),
                  pl.BlockSpec(wo_b.shape, const2),
                  pl.BlockSpec((1, d), const2), pl.BlockSpec((1, d), const2)],
        out_specs=pl.BlockSpec((tm, d), row_map),
        out_shape=jax.ShapeDtypeStruct((R, d), F32),
        compiler_params=_params("parallel"),
        name="merge",
    )(attn, gm, ssm, gates, x, wb_b, wo_b, ln_g, ln_b)


def _ffn_kernel(x_ref, wi_ref, wo_ref, lg_ref, lb_ref, o_ref, hid_ref, *, alpha, tf):
    x = x_ref[...]
    xb = x.astype(BF)
    d_ff = wo_ref.shape[0]
    for c in range(d_ff // tf):
        gate = jnp.dot(xb, wi_ref[:, c * tf:(c + 1) * tf], preferred_element_type=F32)
        up = jnp.dot(xb, wi_ref[:, d_ff + c * tf:d_ff + (c + 1) * tf], preferred_element_type=F32)
        hid_ref[:, c * tf:(c + 1) * tf] = (gate * jax.nn.sigmoid(gate) * up).astype(BF)
    f = jnp.dot(hid_ref[...], wo_ref[...], preferred_element_type=F32)
    o_ref[...] = _layer_norm(alpha * x + f, lg_ref[...], lb_ref[...])


def _ffn(x, w_in_b, w_out_b, ln_g, ln_b, *, tm, tf, alpha):
    R, d = x.shape
    d_ff = w_out_b.shape[0]
    row_map = lambda i: (i, 0)
    const2 = lambda i: (0, 0)
    return pl.pallas_call(
        functools.partial(_ffn_kernel, alpha=alpha, tf=tf),
        grid=(R // tm,),
        in_specs=[pl.BlockSpec((tm, d), row_map),
                  pl.BlockSpec(w_in_b.shape, const2, pipeline_mode=pl.Buffered(1)),
                  pl.BlockSpec(w_out_b.shape, const2, pipeline_mode=pl.Buffered(1)),
                  pl.BlockSpec((1, d), const2), pl.BlockSpec((1, d), const2)],
        out_specs=pl.BlockSpec((tm, d), row_map),
        out_shape=jax.ShapeDtypeStruct((R, d), F32),
        scratch_shapes=[pltpu.VMEM((tm, d_ff), BF)],
        compiler_params=_params("parallel"),
        name="ffn",
    )(x, w_in_b, w_out_b, ln_g, ln_b)


def _rope_tables(pos, hd):
    half = hd // 2
    inv = ROPE_THETA ** (-jnp.arange(half, dtype=F32) / half)
    ang = pos.astype(F32)[:, None] * inv[None, :]
    cos, sin = jnp.cos(ang), jnp.sin(ang)
    reps = LANES // hd
    return (jnp.tile(jnp.concatenate([cos, cos], axis=1), (1, reps)),
            jnp.tile(jnp.concatenate([-sin, sin], axis=1), (1, reps)))


def _ssm_discretize(a_re, a_im, log_dt, b_re, b_im):
    dt = jnp.exp(log_dt.astype(F32))[:, None]
    ar, ai = a_re.astype(F32), a_im.astype(F32)
    mag = jnp.exp(dt * ar)
    abar_re = mag * jnp.cos(dt * ai)
    abar_im = mag * jnp.sin(dt * ai)
    nr, ni = abar_re - 1.0, abar_im
    den = ar * ar + ai * ai
    c_re = (nr * ar + ni * ai) / den
    c_im = (ni * ar - nr * ai) / den
    br, bi = b_re.astype(F32), b_im.astype(F32)
    bbar_re = c_re[..., None] * br - c_im[..., None] * bi
    bbar_im = c_re[..., None] * bi + c_im[..., None] * br
    return abar_re, abar_im, bbar_re, bbar_im


def _ssm_weights(a_re, a_im, log_dt, b_re, b_im, c_re, c_im):
    n_grp, n_state, grp_ch = b_re.shape
    abar_re, abar_im, bbar_re, bbar_im = _ssm_discretize(a_re, a_im, log_dt, b_re, b_im)
    gps = MXU_K // grp_ch
    n_slab = n_grp // gps
    eye = jnp.eye(gps, dtype=F32)

    def in_blocks(bb):
        bb = bb.reshape(n_slab, gps, n_state, grp_ch)
        return jnp.einsum('sgpc,gk->sgckp', bb, eye).reshape(n_slab, gps * grp_ch, gps * n_state)

    def out_blocks(cc):
        cc = cc.reshape(n_slab, gps, grp_ch, n_state)
        return jnp.einsum('sgcp,gk->sgpkc', cc, eye).reshape(n_slab, gps * n_state, gps * grp_ch)

    bw = jnp.concatenate([in_blocks(bbar_re), in_blocks(bbar_im)], axis=2).astype(BF)
    cw = jnp.stack([out_blocks(c_re.astype(F32)), -out_blocks(c_im.astype(F32))]).astype(BF)
    a_flat = jnp.stack([abar_re.reshape(1, -1), abar_im.reshape(1, -1)])
    a2_re = abar_re * abar_re - abar_im * abar_im
    a2_im = 2.0 * abar_re * abar_im
    half = SUBLANES // 2

    def halves(lo, hi):
        return jnp.concatenate([jnp.broadcast_to(lo.reshape(1, -1), (half, lo.size)),
                                jnp.broadcast_to(hi.reshape(1, -1), (half, hi.size))], axis=0)

    zero = jnp.zeros_like(abar_re)
    pq = jnp.stack([halves(abar_re, a2_re), halves(abar_im, a2_im),
                    halves(zero, abar_re), halves(zero, abar_im)])
    return bw, cw, a_flat, pq


def _lambda_init(l):
    return 0.8 - 0.6 * math.exp(-0.3 * l)


def kernel(x_prompt, x_sample, cache_k, cache_v, page_table, state_ssm_re, state_ssm_im, w_in, b_gate, lambda_q1, lambda_k1, lambda_q2, lambda_k2, attn_norm_g, gmlp_ln_g, gmlp_ln_b, gmlp_ws, gmlp_bs, ssm_a_re, ssm_a_im, ssm_log_dt, ssm_b_re, ssm_b_im, ssm_c_re, ssm_c_im, ssm_d, ssm_glu_w, ssm_glu_b, w_branch, w_out, ln1_g, ln1_b, ln2_g, ln2_b, w_ffn_in, w_ffn_out):
    bp, lp, d_model = x_prompt.shape
    bs, ls, _ = x_sample.shape
    depth = w_in.shape[0]
    n_pool, page, n_heads, dv = cache_k.shape[1:]
    hd = dv // 2
    n_pages = page_table.shape[1]
    n_past = n_pages * page
    wbr = ssm_glu_w.shape[1]
    n_grp_g, chunk = gmlp_ws.shape[1], gmlp_ws.shape[2]
    n_grp_s, n_state, grp_ch = ssm_b_re.shape[1:]
    ns = n_grp_s * n_state
    d_ff = w_ffn_out.shape[1]
    alpha = (2 * depth) ** 0.25
    rp, rs = bp * lp, bs * ls
    assert bp == SUBLANES // 2 and n_heads * dv == wbr and wbr % MXU_K == 0 and LANES % hd == 0
    assert dv == LANES and ls <= SUBLANES and rs == chunk and lp % page == 0

    xp = x_prompt.reshape(rp, d_model)
    xs = x_sample.transpose(1, 0, 2).reshape(rs, d_model)
    cos_p, sin_p = _rope_tables(jnp.arange(lp), hd)
    cos_s, sin_s = _rope_tables(jnp.repeat(n_past + jnp.arange(ls), bs), hd)
    ck = cache_k.reshape(depth, n_pool, page * n_heads, dv)
    cv = cache_v.reshape(depth, n_pool, page * n_heads, dv)
    pt_flat = page_table.reshape(-1).astype(jnp.int32)

    tm_p = min(512, lp)
    tk = min(512, lp)
    tf = 256 if d_ff % 256 == 0 else d_ff
    t_chunk = min(256, lp)
    gc = wbr // n_grp_g

    outs = {k: [] for k in ("ks", "vs", "hrp", "hip", "hrs", "his", "gvs")}
    kv_stacked = ()
    for l in range(depth):
        lam_init = _lambda_init(l)
        post_scale = 1.0 - lam_init
        lam = (jnp.exp(jnp.sum(lambda_q1[l].astype(F32) * lambda_k1[l].astype(F32)))
               - jnp.exp(jnp.sum(lambda_q2[l].astype(F32) * lambda_k2[l].astype(F32)))
               + lam_init).reshape(1)
        w_in_b = w_in[l].astype(BF)
        wb_b = w_branch[l].astype(BF)
        wo_b = w_out[l].astype(BF)
        wfi_b = w_ffn_in[l].astype(BF)
        wfo_b = w_ffn_out[l].astype(BF)
        glu_b16 = ssm_glu_w[l].astype(BF)
        an_g = attn_norm_g[l].reshape(1, dv)
        ws_tril = jnp.tril(gmlp_ws[l])
        bw, cw, a_flat, pq = _ssm_weights(ssm_a_re[l], ssm_a_im[l], ssm_log_dt[l], ssm_b_re[l], ssm_b_im[l],
                                          ssm_c_re[l], ssm_c_im[l])
        d_row = ssm_d[l].reshape(1, wbr)
        glu_bias = ssm_glu_b[l].reshape(1, wbr)
        ln_gg, ln_gb = gmlp_ln_g[l].reshape(1, wbr), gmlp_ln_b[l].reshape(1, wbr)
        bgate = b_gate[l].reshape(1, -1)
        l1g, l1b = ln1_g[l].reshape(1, d_model), ln1_b[l].reshape(1, d_model)
        l2g, l2b = ln2_g[l].reshape(1, d_model), ln2_b[l].reshape(1, d_model)

        bias_p = jnp.repeat(gmlp_bs[l].T, gc, axis=1)
        q, kt, kf_all, vx, vf_all, gm, su, gates = _in_proj(
            xp, w_in_b, cos_p, sin_p, bgate, ln_gg, ln_gb, ws_tril.astype(BF), bias_p, kv_stacked,
            wb=wbr, hd=hd, n_heads=n_heads, tm=tm_p, tk=tk, rows_per_seq=lp, prompt=True,
            layer=l, depth=depth)
        kv_stacked = (kf_all, vf_all)
        attn = _attn_prompt(lam, q, kt, vx, an_g, n_seq=bp, seq=lp, n_heads=n_heads, hd=hd, tk=tk,
                            post_scale=post_scale)
        ssm, hre, him = _ssm_prompt(su.reshape(bp, lp, wbr), bw, cw, pq, d_row, glu_b16, glu_bias,
                                    t_chunk=t_chunk)
        x1 = _merge(attn, gm, ssm.reshape(rp, wbr), gates, xp, wb_b, wo_b, l1g, l1b,
                    tm=min(512, rp), alpha=alpha)
        xp = _ffn(x1, wfi_b, wfo_b, l2g, l2b, tm=min(512, rp), tf=tf, alpha=alpha)
        outs["hrp"].append(hre[SUBLANES // 2:].reshape(bp, n_grp_s, n_state))
        outs["hip"].append(him[SUBLANES // 2:].reshape(bp, n_grp_s, n_state))

        ws_s = jnp.einsum('gts,bc->gtbsc', ws_tril[:, :ls, :ls], jnp.eye(bs, dtype=F32)).reshape(n_grp_g, rs, rs)
        bias_s = jnp.repeat(jnp.repeat(gmlp_bs[l][:, :ls].T, bs, axis=0), gc, axis=1)
        q, kb, kf, vb, vf, gm, vn, su, gates = _in_proj(
            xs, w_in_b, cos_s, sin_s, bgate, ln_gg, ln_gb, ws_s.astype(BF), bias_s,
            wb=wbr, hd=hd, n_heads=n_heads, tm=rs, tk=rs, rows_per_seq=rs, prompt=False)
        q3 = q.reshape(ls, bs, wbr).transpose(1, 0, 2)
        zq = jnp.zeros((bs, SUBLANES - ls, wbr), BF)
        q16 = jnp.concatenate([q3, zq, q3, zq], axis=1).reshape(bs * 2 * SUBLANES, wbr)
        pad_new = lambda a: jnp.pad(a.reshape(ls, bs, wbr).transpose(1, 0, 2),
                                    ((0, 0), (0, page - ls), (0, 0))).reshape(bs * page, wbr)
        attn8 = _attn_sample(pt_flat, lam, q16, pad_new(kb), pad_new(vb), an_g, ck, cv,
                             layer=l, n_seq=bs, n_pages=n_pages, n_heads=n_heads, hd=hd,
                             post_scale=post_scale)
        attn = attn8.reshape(bs, SUBLANES, wbr)[:, :ls].transpose(1, 0, 2).reshape(rs, wbr).astype(BF)
        ssm, hre, him = _ssm_sample(su, state_ssm_re[l].reshape(bs, ns), state_ssm_im[l].reshape(bs, ns),
                                    bw, cw, a_flat, d_row, glu_b16, glu_bias, n_steps=ls)
        x1 = _merge(attn, gm, ssm, gates, xs, wb_b, wo_b, l1g, l1b, tm=rs, alpha=alpha)
        xs = _ffn(x1, wfi_b, wfo_b, l2g, l2b, tm=rs, tf=tf, alpha=alpha)
        to_seq_major = lambda a: a.reshape(ls, bs, -1).transpose(1, 0, 2)
        outs["ks"].append(to_seq_major(kf).reshape(bs, ls, n_heads, dv))
        outs["vs"].append(to_seq_major(vf).reshape(bs, ls, n_heads, dv))
        outs["hrs"].append(hre.reshape(bs, n_grp_s, n_state))
        outs["his"].append(him.reshape(bs, n_grp_s, n_state))
        outs["gvs"].append(to_seq_major(vn))

    st = lambda k: jnp.stack(outs[k])
    kv_shape = (depth, bp, lp // page, page, n_heads, dv)
    return (xp.reshape(bp, lp, d_model), xs.reshape(ls, bs, d_model).transpose(1, 0, 2),
            kv_stacked[0].reshape(kv_shape), kv_stacked[1].reshape(kv_shape), st("ks"), st("vs"),
            st("hrp"), st("hip"), st("hrs"), st("his"), st("gvs"))
```

```python
import functools
import math

import jax
import jax.numpy as jnp
from jax import lax
from jax.experimental import pallas as pl
from jax.experimental.pallas import tpu as pltpu

BF = jnp.bfloat16
F32 = jnp.float32

LN_EPS = 1e-5
ROPE_THETA = 10000.0
NEG_BIG = -1e30
LOG2E = math.log2(math.e)
LANES = 128
SUBLANES = 8
MXU_K = 256
VMEM_LIMIT = 56 * 1024 * 1024


def _params(*sem):
    return pltpu.CompilerParams(dimension_semantics=sem, vmem_limit_bytes=VMEM_LIMIT)


def _layer_block(stacked, layer, **kw):
    zeros = (0,) * (stacked.ndim - 1)
    return pl.BlockSpec((None,) + stacked.shape[1:], lambda i: (layer,) + zeros, **kw)


def _layer_norm(r, g, b):
    mu = jnp.mean(r, axis=-1, keepdims=True)
    d = r - mu
    var = jnp.mean(d * d, axis=-1, keepdims=True)
    return d * lax.rsqrt(var + LN_EPS) * g + b


def _in_proj_kernel(*refs, hd, n_heads, q_scale, prompt, tk, n_alias):
    x_ref, w_ref, cos_ref, sin_ref, bg_ref, lng_ref, lnb_ref, ws_ref, bias_ref = refs[:9]
    rest = refs[9 + n_alias:]
    if prompt:
        q_ref, kt_ref, kf_ref, vx_ref, vf_ref, gm_ref, su_ref, gates_ref = rest
    else:
        q_ref, kb_ref, kf_ref, vb_ref, vf_ref, gm_ref, vn_ref, su_ref, gates_ref = rest
    tm = x_ref.shape[0]
    wb = q_ref.shape[1]
    dv = 2 * hd
    half = hd // 2
    xb = x_ref[...].astype(BF)

    def col(c):
        return jnp.dot(xb, w_ref[:, c * wb:(c + 1) * wb], preferred_element_type=F32)

    def rope(a):
        cos = cos_ref[...]
        sin = sin_ref[...]
        outs = []
        for c in range(a.shape[1] // LANES):
            blk = a[:, c * LANES:(c + 1) * LANES]
            lane = lax.broadcasted_iota(jnp.int32, blk.shape, 1)
            swapped = jnp.where((lane % hd) < half,
                                pltpu.roll(blk, LANES - half, 1), pltpu.roll(blk, half, 1))
            outs.append(blk * cos + swapped * sin)
        return jnp.concatenate(outs, axis=1)

    def store_head_rows(dst_ref, a):
        for h in range(n_heads):
            dst_ref[pl.ds(h, tm, stride=n_heads), :] = a[:, h * dv:(h + 1) * dv]

    q_ref[...] = (rope(col(0)) * q_scale).astype(BF)

    k = rope(col(1))
    if prompt:
        store_head_rows(kf_ref, k)
        for t in range(tm // tk):
            kt_ref[t] = k[t * tk:(t + 1) * tk, :].T.astype(BF)
    else:
        kf_ref[...] = k
        kb_ref[...] = k.astype(BF)

    v = col(2)
    if prompt:
        store_head_rows(vf_ref, v)
        ones = jnp.ones((tm, dv), BF)
        for h in range(n_heads):
            vx_ref[:, 2 * h * dv:(2 * h + 1) * dv] = v[:, h * dv:(h + 1) * dv].astype(BF)
            vx_ref[:, (2 * h + 1) * dv:(2 * h + 2) * dv] = ones
    else:
        vf_ref[...] = v
        vb_ref[...] = v.astype(BF)

    vn = _layer_norm(jax.nn.gelu(col(4)), lng_ref[...], lnb_ref[...])
    if not prompt:
        vn_ref[...] = vn
    vb16 = vn.astype(BF)
    u = jax.nn.gelu(col(3))
    n_grp, chunk = ws_ref.shape[0], ws_ref.shape[1]
    gc = wb // n_grp
    for n in range(tm // chunk):
        rs = slice(n * chunk, (n + 1) * chunk)
        for g in range(n_grp):
            cs = slice(g * gc, (g + 1) * gc)
            mixed = jnp.dot(ws_ref[g], vb16[rs, cs], preferred_element_type=F32) + bias_ref[:, cs]
            gm_ref[rs, cs] = (u[rs, cs] * mixed).astype(BF)

    su_ref[...] = col(5).astype(BF)

    for n in range(gates_ref.shape[1] // wb):
        cs = slice(n * wb, (n + 1) * wb)
        gates_ref[:, cs] = jax.nn.sigmoid(col(6 + n) + bg_ref[:, cs]).astype(BF)


def _in_proj(x, w_b, cos_t, sin_t, b_gate, ln_g, ln_b, ws_b, bias_full, kv_prev=(), *,
             wb, hd, n_heads, tm, tk, rows_per_seq, prompt, layer=0, depth=1):
    R, D = x.shape
    n_gate = w_b.shape[2] // wb - 6
    nt = rows_per_seq // tm
    dv = 2 * hd
    chunk = ws_b.shape[1]
    row_map = lambda i: (i, 0)
    const2 = lambda i: (0, 0)
    tab_map = lambda i: (i % nt, 0)
    blk = lambda n=1: pl.BlockSpec((tm, n * wb), row_map)
    sds = jax.ShapeDtypeStruct
    if prompt:
        stacked = sds((depth, R * n_heads, dv), F32)
        stacked_spec = pl.BlockSpec((None, tm * n_heads, dv), lambda i: (layer, i, 0))
        out_shape = (sds((R, wb), BF), sds((R // tk, wb, tk), BF), stacked, sds((R, 2 * wb), BF), stacked,
                     sds((R, wb), BF), sds((R, wb), BF), sds((R, n_gate * wb), BF))
        out_specs = (blk(), pl.BlockSpec((tm // tk, wb, tk), lambda i: (i, 0, 0)), stacked_spec,
                     blk(2), stacked_spec, blk(), blk(), blk(n_gate))
        aliases = {9 + n: (2, 4)[n] for n in range(len(kv_prev))}
    else:
        out_shape = (sds((R, wb), BF), sds((R, wb), BF), sds((R, wb), F32), sds((R, wb), BF), sds((R, wb), F32),
                     sds((R, wb), BF), sds((R, wb), F32), sds((R, wb), BF), sds((R, n_gate * wb), BF))
        out_specs = (blk(),) * 8 + (blk(n_gate),)
        aliases = {}
    return pl.pallas_call(
        functools.partial(_in_proj_kernel, hd=hd, n_heads=n_heads, q_scale=hd ** -0.5 * LOG2E,
                          prompt=prompt, tk=tk, n_alias=len(kv_prev)),
        grid=(R // tm,),
        in_specs=[pl.BlockSpec((tm, D), row_map),
                  _layer_block(w_b, layer, pipeline_mode=pl.Buffered(1)),
                  pl.BlockSpec((tm, LANES), tab_map),
                  pl.BlockSpec((tm, LANES), tab_map),
                  pl.BlockSpec((1, n_gate * wb), const2),
                  pl.BlockSpec((1, wb), const2), pl.BlockSpec((1, wb), const2),
                  pl.BlockSpec(ws_b.shape, lambda i: (0, 0, 0)),
                  pl.BlockSpec((chunk, wb), const2)]
                 + [pl.BlockSpec(memory_space=pl.ANY)] * len(kv_prev),
        out_specs=out_specs,
        out_shape=out_shape,
        input_output_aliases=aliases,
        compiler_params=_params("parallel"),
        name="in_proj_prompt" if prompt else "in_proj_sample",
    )(x, w_b, cos_t, sin_t, b_gate, ln_g, ln_b, ws_b, bias_full, *kv_prev)


def _attn_prompt_kernel(lam_ref, q_ref, kt_ref, vx_ref, g_ref, o_ref, qs_ref, m_ref, acc_ref,
                        *, tq, tk, hd, n_heads, post_scale, unroll):
    qi = pl.program_id(1)
    dv = 2 * hd
    for h in range(n_heads):
        q = q_ref[:, h * dv:(h + 1) * dv].astype(F32)
        lane = lax.broadcasted_iota(jnp.int32, q.shape, 1)
        qs_ref[h, 0:tq, :] = jnp.where(lane < hd, q, 0.0).astype(BF)
        qs_ref[h, tq:2 * tq, :] = jnp.where(lane >= hd, q, 0.0).astype(BF)
    m_ref[...] = jnp.full(m_ref.shape, -jnp.inf, F32)
    acc_ref[...] = jnp.zeros(acc_ref.shape, F32)

    def tile(ki, masked):
        k0 = pl.multiple_of(ki * tk, tk)
        for h in range(n_heads):
            s = jnp.dot(qs_ref[h], kt_ref[ki, h * dv:(h + 1) * dv, :], preferred_element_type=F32)
            if masked:
                row = lax.broadcasted_iota(jnp.int32, s.shape, 0)
                col = lax.broadcasted_iota(jnp.int32, s.shape, 1)
                s = jnp.where(ki * tk + col <= qi * tq + row % tq, s, NEG_BIG)
            m_prev = m_ref[h]
            m_new = jnp.maximum(m_prev, jnp.max(s, axis=1, keepdims=True))
            alpha = jnp.exp2(m_prev - m_new)
            p = jnp.exp2(s - m_new[:, :1]).astype(BF)
            pv = jnp.dot(p, vx_ref[pl.ds(k0, tk), 2 * h * dv:(2 * h + 2) * dv], preferred_element_type=F32)
            acc_ref[h, :, 0:dv] = alpha * acc_ref[h, :, 0:dv] + pv[:, 0:dv]
            acc_ref[h, :, dv:2 * dv] = alpha * acc_ref[h, :, dv:2 * dv] + pv[:, dv:2 * dv]
            m_ref[h] = m_new

    n_full = (qi * tq) // tk
    n_groups = n_full // unroll
    tail = n_full + 1 - n_groups * unroll

    def full_group(kg, carry):
        for u in range(unroll):
            tile(unroll * kg + u, False)
        return carry

    lax.fori_loop(0, n_groups, full_group, 0)
    for r in range(1, unroll + 1):
        @pl.when(tail == r)
        def _(r=r):
            for u in range(r - 1):
                tile(n_groups * unroll + u, False)
            tile(n_groups * unroll + r - 1, True)

    for h in range(n_heads):
        t = acc_ref[h, :, 0:dv] / acc_ref[h, :, dv:2 * dv]
        o = t[0:tq] - lam_ref[0] * t[tq:2 * tq]
        ms = jnp.mean(o * o, axis=1, keepdims=True)
        o_ref[:, h * dv:(h + 1) * dv] = (o * lax.rsqrt(ms + LN_EPS) * g_ref[...] * post_scale).astype(BF)


def _attn_prompt(lam, q, kt, vx, g, *, n_seq, seq, n_heads, hd, tk, post_scale):
    tq = min(256, seq)
    dv = 2 * hd
    w = n_heads * dv
    nq = seq // tq
    return pl.pallas_call(
        functools.partial(_attn_prompt_kernel, tq=tq, tk=tk, hd=hd, n_heads=n_heads, post_scale=post_scale,
                          unroll=3),
        grid=(n_seq, nq),
        in_specs=[pl.BlockSpec(memory_space=pltpu.SMEM),
                  pl.BlockSpec((tq, w), lambda b, i: (b * nq + i, 0)),
                  pl.BlockSpec((seq // tk, w, tk), lambda b, i: (b, 0, 0)),
                  pl.BlockSpec((seq, 2 * w), lambda b, i: (b, 0)),
                  pl.BlockSpec((1, dv), lambda b, i: (0, 0))],
        out_specs=pl.BlockSpec((tq, w), lambda b, i: (b * nq + i, 0)),
        out_shape=jax.ShapeDtypeStruct(q.shape, BF),
        scratch_shapes=[pltpu.VMEM((n_heads, 2 * tq, dv), BF), pltpu.VMEM((n_heads, 2 * tq, dv), F32),
                        pltpu.VMEM((n_heads, 2 * tq, 2 * dv), F32)],
        compiler_params=_params("parallel", "arbitrary"),
        name="attn_prompt",
    )(lam, q, kt, vx, g)


def _attn_sample_kernel(pt_ref, lam_ref, q_ref, kn_ref, vn_ref, g_ref, *rest,
                        n_pg, n_heads, hd, post_scale):
    k_refs = rest[:n_pg]
    v_refs = rest[n_pg:2 * n_pg]
    o_ref, qp_ref, m_ref, l_ref, acc_ref = rest[2 * n_pg:]
    j = pl.program_id(1)
    dv = 2 * hd
    nr = q_ref.shape[0]
    page = k_refs[0].shape[0] // n_heads
    n_pairs = n_heads // 2
    nt_dims = (((1,), (1,)), ((), ()))

    @pl.when(j == 0)
    def _():
        for a in range(n_pairs):
            for side in range(2):
                h = 2 * a + side
                qh = q_ref[:, h * dv:(h + 1) * dv].astype(F32)
                row = lax.broadcasted_iota(jnp.int32, qh.shape, 0)
                lane = lax.broadcasted_iota(jnp.int32, qh.shape, 1)
                keep = (row < nr // 2) == (lane < hd)
                qm = jnp.where(keep, qh, 0.0).astype(BF)
                zero = jnp.zeros_like(qm)
                qp_ref[a, side * nr:(side + 1) * nr, :] = jnp.concatenate(
                    [qm, zero] if side == 0 else [zero, qm], axis=1)
        m_ref[...] = jnp.full(m_ref.shape, -jnp.inf, F32)
        l_ref[...] = jnp.zeros(l_ref.shape, F32)
        acc_ref[...] = jnp.zeros(acc_ref.shape, F32)

    def update(a, s, pv):
        m_prev = m_ref[a]
        m_new = jnp.maximum(m_prev, jnp.max(s, axis=1, keepdims=True))
        alpha = jnp.exp2(m_prev - m_new)
        p = jnp.exp2(s - m_new[:, :1])
        l_ref[a] = alpha * l_ref[a] + jnp.sum(p, axis=1, keepdims=True)
        acc_ref[a] = jnp.concatenate([alpha, alpha], axis=1) * acc_ref[a] + pv(p.astype(BF))
        m_ref[a] = m_new

    def pair_rows(page_refs, a):
        def head(h):
            return jnp.concatenate([r[pl.ds(h, page, stride=n_heads), :] for r in page_refs], axis=0)
        return jnp.concatenate([head(2 * a), head(2 * a + 1)], axis=1).astype(BF)

    for a in range(n_pairs):
        s = lax.dot_general(qp_ref[a], pair_rows(k_refs, a), nt_dims, preferred_element_type=F32)
        update(a, s, lambda p, a=a: jnp.dot(p, pair_rows(v_refs, a), preferred_element_type=F32))

    @pl.when(j == pl.num_programs(1) - 1)
    def _():
        for a in range(n_pairs):
            ps = slice(2 * a * dv, (2 * a + 2) * dv)
            s = lax.dot_general(qp_ref[a], kn_ref[:, ps], nt_dims, preferred_element_type=F32)
            row = lax.broadcasted_iota(jnp.int32, s.shape, 0)
            col = lax.broadcasted_iota(jnp.int32, s.shape, 1)
            s = jnp.where(col <= row % (nr // 2), s, NEG_BIG)
            update(a, s, lambda p, ps=ps: jnp.dot(p, vn_ref[:, ps], preferred_element_type=F32))
            for side in range(2):
                h = 2 * a + side
                t = (acc_ref[a, side * nr:(side + 1) * nr, side * dv:(side + 1) * dv]
                     / l_ref[a, side * nr:(side + 1) * nr, :])
                o = t[0:nr // 2] - lam_ref[0] * t[nr // 2:nr]
                ms = jnp.mean(o * o, axis=1, keepdims=True)
                o_ref[:, h * dv:(h + 1) * dv] = o * lax.rsqrt(ms + LN_EPS) * g_ref[...] * post_scale


def _attn_sample(page_table_flat, lam, q16, kn, vn, g, cache_k, cache_v, *,
                 layer, n_seq, n_pages, n_heads, hd, post_scale):
    n_pg = math.gcd(16, n_pages)
    nr = 2 * SUBLANES
    dv = 2 * hd
    w = n_heads * dv
    page_rows = cache_k.shape[2]
    page = page_rows // n_heads

    def kv_spec(g):
        return pl.BlockSpec((None, None, page_rows, dv),
                            lambda b, j, pt: (layer, pt[b * n_pages + j * n_pg + g], 0, 0))

    grid_spec = pltpu.PrefetchScalarGridSpec(
        num_scalar_prefetch=1,
        grid=(n_seq, n_pages // n_pg),
        in_specs=[pl.BlockSpec(memory_space=pltpu.SMEM),
                  pl.BlockSpec((nr, w), lambda b, j, pt: (b, 0)),
                  pl.BlockSpec((page, w), lambda b, j, pt: (b, 0)),
                  pl.BlockSpec((page, w), lambda b, j, pt: (b, 0)),
                  pl.BlockSpec((1, dv), lambda b, j, pt: (0, 0))]
                 + [kv_spec(g) for g in range(n_pg)] + [kv_spec(g) for g in range(n_pg)],
        out_specs=pl.BlockSpec((nr // 2, w), lambda b, j, pt: (b, 0)),
        scratch_shapes=[pltpu.VMEM((n_heads // 2, 2 * nr, 2 * dv), BF),
                        pltpu.VMEM((n_heads // 2, 2 * nr, dv), F32), pltpu.VMEM((n_heads // 2, 2 * nr, dv), F32),
                        pltpu.VMEM((n_heads // 2, 2 * nr, 2 * dv), F32)])
    return pl.pallas_call(
        functools.partial(_attn_sample_kernel, n_pg=n_pg, n_heads=n_heads, hd=hd, post_scale=post_scale),
        grid_spec=grid_spec,
        out_shape=jax.ShapeDtypeStruct((n_seq * (nr // 2), w), F32),
        compiler_params=_params("parallel", "arbitrary"),
        name="attn_sample",
    )(page_table_flat, lam, q16, kn, vn, g, *([cache_k] * n_pg), *([cache_v] * n_pg))


def _ssm_input_proj(ub, bw_ref, xre_ref, xim_ref):
    n_slab = bw_ref.shape[0]
    ns = xre_ref.shape[1] // n_slab
    for h in range(n_slab):
        uh = ub[:, h * MXU_K:(h + 1) * MXU_K]
        xre_ref[:, h * ns:(h + 1) * ns] = jnp.dot(uh, bw_ref[h, :, 0:ns], preferred_element_type=F32)
        xim_ref[:, h * ns:(h + 1) * ns] = jnp.dot(uh, bw_ref[h, :, ns:2 * ns], preferred_element_type=F32)


def _ssm_output(u, hre_ref, him_ref, cw_ref, d_ref, gw_ref, gb_ref):
    n_slab = cw_ref.shape[1]
    ns = hre_ref.shape[1] // n_slab
    ys = []
    for h in range(n_slab):
        ss = slice(h * ns, (h + 1) * ns)
        ys.append(jnp.dot(hre_ref[:, ss].astype(BF), cw_ref[0, h], preferred_element_type=F32)
                  + jnp.dot(him_ref[:, ss].astype(BF), cw_ref[1, h], preferred_element_type=F32))
    y = jnp.concatenate(ys, axis=1) + d_ref[...] * u
    a = jax.nn.gelu(y)
    z = jnp.dot(a.astype(BF), gw_ref[...], preferred_element_type=F32) + gb_ref[...]
    return a * jax.nn.sigmoid(z)


def _ssm_prompt_kernel(u_ref, bw_ref, cw_ref, pq_ref, d_ref, gw_ref, gb_ref,
                       out_ref, hre_out, him_out, utb_ref, xre_ref, xim_ref, cre_ref, cim_ref, *, lane_grp):
    @pl.when(pl.program_id(0) == 0)
    def _():
        cre_ref[...] = jnp.zeros(cre_ref.shape, F32)
        cim_ref[...] = jnp.zeros(cim_ref.shape, F32)

    n_seq, t_chunk = u_ref.shape[0], u_ref.shape[1]
    n_slab = utb_ref.shape[0]
    for b in range(n_seq):
        ub = u_ref[b].astype(F32)
        for c in range(n_slab):
            utb_ref[c, pl.ds(b, t_chunk, stride=n_seq), :] = ub[:, c * LANES:(c + 1) * LANES]
    u = jnp.concatenate([utb_ref[c] for c in range(n_slab)], axis=1)
    ub16 = u.astype(BF)

    rows, ns = xre_ref.shape
    half = SUBLANES // 2
    ns_slab = ns // bw_ref.shape[0]
    n_lg = ns // lane_grp

    def input_proj(lg):
        slab, off = divmod(lg * lane_grp, ns_slab)
        uh = ub16[:, slab * MXU_K:(slab + 1) * MXU_K]
        sl = slice(lg * lane_grp, (lg + 1) * lane_grp)
        xre_ref[:, sl] = jnp.dot(uh, bw_ref[slab, :, off:off + lane_grp], preferred_element_type=F32)
        xim_ref[:, sl] = jnp.dot(uh, bw_ref[slab, :, ns_slab + off:ns_slab + off + lane_grp],
                                 preferred_element_type=F32)

    input_proj(0)
    for lg in range(n_lg):
        if lg + 1 < n_lg:
            input_proj(lg + 1)
        sl = slice(lg * lane_grp, (lg + 1) * lane_grp)
        p_re, p_im, q_re, q_im = pq_ref[0, :, sl], pq_ref[1, :, sl], pq_ref[2, :, sl], pq_ref[3, :, sl]
        upper = lax.broadcasted_iota(jnp.int32, (SUBLANES, lane_grp), 0) >= half
        c_re, c_im = cre_ref[:, sl], cim_ref[:, sl]
        for i in range(rows // SUBLANES):
            rs = slice(i * SUBLANES, (i + 1) * SUBLANES)
            x_re = xre_ref[rs, sl]
            x_im = xim_ref[rs, sl]
            s_re = pltpu.roll(x_re, half, 0)
            s_im = pltpu.roll(x_im, half, 0)
            h_re = x_re + q_re * s_re - q_im * s_im + p_re * c_re - p_im * c_im
            h_im = x_im + q_re * s_im + q_im * s_re + p_re * c_im + p_im * c_re
            xre_ref[rs, sl] = h_re
            xim_ref[rs, sl] = h_im
            c_re = jnp.where(upper, h_re, pltpu.roll(h_re, half, 0))
            c_im = jnp.where(upper, h_im, pltpu.roll(h_im, half, 0))
        cre_ref[:, sl] = c_re
        cim_ref[:, sl] = c_im

    y = _ssm_output(u, xre_ref, xim_ref, cw_ref, d_ref, gw_ref, gb_ref)
    for c in range(n_slab):
        utb_ref[c] = y[:, c * LANES:(c + 1) * LANES]
    for b in range(n_seq):
        for c in range(n_slab):
            out_ref[b, :, c * LANES:(c + 1) * LANES] = utb_ref[c, pl.ds(b, t_chunk, stride=n_seq), :].astype(BF)
    hre_out[...] = cre_ref[...]
    him_out[...] = cim_ref[...]


def _ssm_prompt(su, bw, cw, pq, d, gw, gb, *, t_chunk):
    n_seq, seq, w = su.shape
    ns = pq.shape[2]
    rows = n_seq * t_chunk
    blk_map = lambda i: (0, i, 0)
    const2 = lambda i: (0, 0)
    return pl.pallas_call(
        functools.partial(_ssm_prompt_kernel, lane_grp=min(512, ns)),
        grid=(seq // t_chunk,),
        in_specs=[pl.BlockSpec((n_seq, t_chunk, w), blk_map),
                  pl.BlockSpec(bw.shape, lambda i: (0, 0, 0)),
                  pl.BlockSpec(cw.shape, lambda i: (0, 0, 0, 0)),
                  pl.BlockSpec(pq.shape, lambda i: (0, 0, 0)),
                  pl.BlockSpec((1, w), const2),
                  pl.BlockSpec(gw.shape, const2),
                  pl.BlockSpec((1, w), const2)],
        out_specs=(pl.BlockSpec((n_seq, t_chunk, w), blk_map),
                   pl.BlockSpec((SUBLANES, ns), const2), pl.BlockSpec((SUBLANES, ns), const2)),
        out_shape=(jax.ShapeDtypeStruct((n_seq, seq, w), BF),
                   jax.ShapeDtypeStruct((SUBLANES, ns), F32), jax.ShapeDtypeStruct((SUBLANES, ns), F32)),
        scratch_shapes=[pltpu.VMEM((w // LANES, rows, LANES), F32),
                        pltpu.VMEM((rows, ns), F32), pltpu.VMEM((rows, ns), F32),
                        pltpu.VMEM((SUBLANES, ns), F32), pltpu.VMEM((SUBLANES, ns), F32)],
        compiler_params=_params("arbitrary"),
        name="ssm_prompt",
    )(su, bw, cw, pq, d, gw, gb)


def _ssm_sample_kernel(u_ref, h0re_ref, h0im_ref, bw_ref, cw_ref, a_ref, d_ref, gw_ref, gb_ref,
                       out_ref, hre_out, him_out, xre_ref, xim_ref, *, n_steps):
    ub = u_ref[...]
    _ssm_input_proj(ub, bw_ref, xre_ref, xim_ref)
    nb = h0re_ref.shape[0]
    a_re, a_im = a_ref[0], a_ref[1]
    h_re, h_im = h0re_ref[...], h0im_ref[...]
    for t in range(n_steps):
        rs = slice(t * nb, (t + 1) * nb)
        h_re, h_im = (a_re * h_re - a_im * h_im + xre_ref[rs, :],
                      a_re * h_im + a_im * h_re + xim_ref[rs, :])
        xre_ref[rs, :] = h_re
        xim_ref[rs, :] = h_im
    out_ref[...] = _ssm_output(ub.astype(F32), xre_ref, xim_ref, cw_ref, d_ref, gw_ref, gb_ref).astype(BF)
    hre_out[...] = h_re
    him_out[...] = h_im


def _ssm_sample(su, h0_re, h0_im, bw, cw, a, d, gw, gb, *, n_steps):
    rows, w = su.shape
    nb, ns = h0_re.shape
    full = lambda arr: pl.BlockSpec(arr.shape, lambda i, n=arr.ndim: (0,) * n)
    return pl.pallas_call(
        functools.partial(_ssm_sample_kernel, n_steps=n_steps),
        grid=(1,),
        in_specs=[full(su), full(h0_re), full(h0_im), full(bw), full(cw), full(a), full(d), full(gw), full(gb)],
        out_specs=(pl.BlockSpec((rows, w), lambda i: (0, 0)),
                   pl.BlockSpec((nb, ns), lambda i: (0, 0)), pl.BlockSpec((nb, ns), lambda i: (0, 0))),
        out_shape=(jax.ShapeDtypeStruct((rows, w), BF),
                   jax.ShapeDtypeStruct((nb, ns), F32), jax.ShapeDtypeStruct((nb, ns), F32)),
        scratch_shapes=[pltpu.VMEM((rows, ns), F32), pltpu.VMEM((rows, ns), F32)],
        compiler_params=_params("arbitrary"),
        name="ssm_sample",
    )(su, h0_re, h0_im, bw, cw, a, d, gw, gb)


def _merge_kernel(at_ref, gm_ref, ss_ref, gates_ref, x_ref, wb_ref, wo_ref, lg_ref, lb_ref, o_ref, *, alpha):
    d = x_ref.shape[1]
    merged = None
    for n, br in enumerate((at_ref, gm_ref, ss_ref)):
        proj = jnp.dot(br[...], wb_ref[n], preferred_element_type=F32)
        gated = gates_ref[:, n * d:(n + 1) * d].astype(F32) * proj
        merged = gated if merged is None else merged + gated
    mix = jnp.dot(merged.astype(BF), wo_ref[...], preferred_element_type=F32)
    o_ref[...] = _layer_norm(alpha * x_ref[...] + mix, lg_ref[...], lb_ref[...])


def _merge(attn, gm, ssm, gates, x, wb_b, wo_b, ln_g, ln_b, *, tm, alpha, layer):
    R, d = x.shape
    w = attn.shape[1]
    row_map = lambda i: (i, 0)
    const2 = lambda i: (0, 0)
    return pl.pallas_call(
        functools.partial(_merge_kernel, alpha=alpha),
        grid=(R // tm,),
        in_specs=[pl.BlockSpec((tm, w), row_map), pl.BlockSpec((tm, w), row_map),
                  pl.BlockSpec((tm, w), row_map),
                  pl.BlockSpec((tm, gates.shape[1]), row_map),
                  pl.BlockSpec((tm, d), row_map),
                  _layer_block(wb_b, layer), _layer_block(wo_b, layer),
                  pl.BlockSpec((1, d), const2), pl.BlockSpec((1, d), const2)],
        out_specs=pl.BlockSpec((tm, d), row_map),
        out_shape=jax.ShapeDtypeStruct((R, d), F32),
        compiler_params=_params("parallel"),
        name="merge",
    )(attn, gm, ssm, gates, x, wb_b, wo_b, ln_g, ln_b)


def _ffn_kernel(x_ref, wi_ref, wo_ref, lg_ref, lb_ref, o_ref, hid_ref, *, alpha, tf):
    x = x_ref[...]
    xb = x.astype(BF)
    d_ff = wo_ref.shape[0]
    for c in range(d_ff // tf):
        gate = jnp.dot(xb, wi_ref[:, c * tf:(c + 1) * tf], preferred_element_type=F32)
        up = jnp.dot(xb, wi_ref[:, d_ff + c * tf:d_ff + (c + 1) * tf], preferred_element_type=F32)
        hid_ref[:, c * tf:(c + 1) * tf] = (gate * jax.nn.sigmoid(gate) * up).astype(BF)
    f = jnp.dot(hid_ref[...], wo_ref[...], preferred_element_type=F32)
    o_ref[...] = _layer_norm(alpha * x + f, lg_ref[...], lb_ref[...])


def _ffn(x, w_in_b, w_out_b, ln_g, ln_b, *, tm, tf, alpha, layer):
    R, d = x.shape
    d_ff = w_out_b.shape[1]
    row_map = lambda i: (i, 0)
    const2 = lambda i: (0, 0)
    return pl.pallas_call(
        functools.partial(_ffn_kernel, alpha=alpha, tf=tf),
        grid=(R // tm,),
        in_specs=[pl.BlockSpec((tm, d), row_map),
                  _layer_block(w_in_b, layer, pipeline_mode=pl.Buffered(1)),
                  _layer_block(w_out_b, layer, pipeline_mode=pl.Buffered(1)),
                  pl.BlockSpec((1, d), const2), pl.BlockSpec((1, d), const2)],
        out_specs=pl.BlockSpec((tm, d), row_map),
        out_shape=jax.ShapeDtypeStruct((R, d), F32),
        scratch_shapes=[pltpu.VMEM((tm, d_ff), BF)],
        compiler_params=_params("parallel"),
        name="ffn",
    )(x, w_in_b, w_out_b, ln_g, ln_b)


def _rope_tables(pos, hd):
    half = hd // 2
    inv = ROPE_THETA ** (-jnp.arange(half, dtype=F32) / half)
    ang = pos.astype(F32)[:, None] * inv[None, :]
    cos, sin = jnp.cos(ang), jnp.sin(ang)
    reps = LANES // hd
    return (jnp.tile(jnp.concatenate([cos, cos], axis=1), (1, reps)),
            jnp.tile(jnp.concatenate([-sin, sin], axis=1), (1, reps)))


def _ssm_discretize(a_re, a_im, log_dt, b_re, b_im):
    dt = jnp.exp(log_dt.astype(F32))[:, None]
    ar, ai = a_re.astype(F32), a_im.astype(F32)
    mag = jnp.exp(dt * ar)
    abar_re = mag * jnp.cos(dt * ai)
    abar_im = mag * jnp.sin(dt * ai)
    nr, ni = abar_re - 1.0, abar_im
    den = ar * ar + ai * ai
    c_re = (nr * ar + ni * ai) / den
    c_im = (ni * ar - nr * ai) / den
    br, bi = b_re.astype(F32), b_im.astype(F32)
    bbar_re = c_re[..., None] * br - c_im[..., None] * bi
    bbar_im = c_re[..., None] * bi + c_im[..., None] * br
    return abar_re, abar_im, bbar_re, bbar_im


def _ssm_weights(a_re, a_im, log_dt, b_re, b_im, c_re, c_im):
    n_grp, n_state, grp_ch = b_re.shape
    abar_re, abar_im, bbar_re, bbar_im = _ssm_discretize(a_re, a_im, log_dt, b_re, b_im)
    gps = MXU_K // grp_ch
    n_slab = n_grp // gps
    eye = jnp.eye(gps, dtype=F32)

    def in_blocks(bb):
        bb = bb.reshape(n_slab, gps, n_state, grp_ch)
        return jnp.einsum('sgpc,gk->sgckp', bb, eye).reshape(n_slab, gps * grp_ch, gps * n_state)

    def out_blocks(cc):
        cc = cc.reshape(n_slab, gps, grp_ch, n_state)
        return jnp.einsum('sgcp,gk->sgpkc', cc, eye).reshape(n_slab, gps * n_state, gps * grp_ch)

    bw = jnp.concatenate([in_blocks(bbar_re), in_blocks(bbar_im)], axis=2).astype(BF)
    cw = jnp.stack([out_blocks(c_re.astype(F32)), -out_blocks(c_im.astype(F32))]).astype(BF)
    a_flat = jnp.stack([abar_re.reshape(1, -1), abar_im.reshape(1, -1)])
    a2_re = abar_re * abar_re - abar_im * abar_im
    a2_im = 2.0 * abar_re * abar_im
    half = SUBLANES // 2

    def halves(lo, hi):
        return jnp.concatenate([jnp.broadcast_to(lo.reshape(1, -1), (half, lo.size)),
                                jnp.broadcast_to(hi.reshape(1, -1), (half, hi.size))], axis=0)

    zero = jnp.zeros_like(abar_re)
    pq = jnp.stack([halves(abar_re, a2_re), halves(abar_im, a2_im),
                    halves(zero, abar_re), halves(zero, abar_im)])
    return bw, cw, a_flat, pq


def _lambda_init(l):
    return 0.8 - 0.6 * math.exp(-0.3 * l)


def kernel(x_prompt, x_sample, cache_k, cache_v, page_table, state_ssm_re, state_ssm_im, w_in, b_gate, lambda_q1, lambda_k1, lambda_q2, lambda_k2, attn_norm_g, gmlp_ln_g, gmlp_ln_b, gmlp_ws, gmlp_bs, ssm_a_re, ssm_a_im, ssm_log_dt, ssm_b_re, ssm_b_im, ssm_c_re, ssm_c_im, ssm_d, ssm_glu_w, ssm_glu_b, w_branch, w_out, ln1_g, ln1_b, ln2_g, ln2_b, w_ffn_in, w_ffn_out):
    bp, lp, d_model = x_prompt.shape
    bs, ls, _ = x_sample.shape
    depth = w_in.shape[0]
    n_pool, page, n_heads, dv = cache_k.shape[1:]
    hd = dv // 2
    n_pages = page_table.shape[1]
    n_past = n_pages * page
    wbr = ssm_glu_w.shape[1]
    n_grp_g, chunk = gmlp_ws.shape[1], gmlp_ws.shape[2]
    n_grp_s, n_state, grp_ch = ssm_b_re.shape[1:]
    ns = n_grp_s * n_state
    d_ff = w_ffn_out.shape[1]
    alpha = (2 * depth) ** 0.25
    rp, rs = bp * lp, bs * ls
    assert bp == SUBLANES // 2 and n_heads * dv == wbr and wbr % MXU_K == 0 and LANES % hd == 0
    assert dv == LANES and ls <= SUBLANES and rs == chunk and lp % page == 0

    xp = x_prompt.reshape(rp, d_model)
    xs = x_sample.transpose(1, 0, 2).reshape(rs, d_model)
    cos_p, sin_p = _rope_tables(jnp.arange(lp), hd)
    cos_s, sin_s = _rope_tables(jnp.repeat(n_past + jnp.arange(ls), bs), hd)
    ck = cache_k.reshape(depth, n_pool, page * n_heads, dv)
    cv = cache_v.reshape(depth, n_pool, page * n_heads, dv)
    pt_flat = page_table.reshape(-1).astype(jnp.int32)

    tm_p = min(512, lp)
    tk = min(512, lp)
    tf = 256 if d_ff % 256 == 0 else d_ff
    t_chunk = min(256, lp)
    gc = wbr // n_grp_g

    w_in_b, wb_b, wo_b = w_in.astype(BF), w_branch.astype(BF), w_out.astype(BF)
    wfi_b, wfo_b = w_ffn_in.astype(BF), w_ffn_out.astype(BF)

    outs = {k: [] for k in ("ks", "vs", "hrp", "hip", "hrs", "his", "gvs")}
    kv_stacked = ()
    for l in range(depth):
        lam_init = _lambda_init(l)
        post_scale = 1.0 - lam_init
        lam = (jnp.exp(jnp.sum(lambda_q1[l].astype(F32) * lambda_k1[l].astype(F32)))
               - jnp.exp(jnp.sum(lambda_q2[l].astype(F32) * lambda_k2[l].astype(F32)))
               + lam_init).reshape(1)
        glu_b16 = ssm_glu_w[l].astype(BF)
        an_g = attn_norm_g[l].reshape(1, dv)
        ws_tril = jnp.tril(gmlp_ws[l])
        bw, cw, a_flat, pq = _ssm_weights(ssm_a_re[l], ssm_a_im[l], ssm_log_dt[l], ssm_b_re[l], ssm_b_im[l],
                                          ssm_c_re[l], ssm_c_im[l])
        d_row = ssm_d[l].reshape(1, wbr)
        glu_bias = ssm_glu_b[l].reshape(1, wbr)
        ln_gg, ln_gb = gmlp_ln_g[l].reshape(1, wbr), gmlp_ln_b[l].reshape(1, wbr)
        bgate = b_gate[l].reshape(1, -1)
        l1g, l1b = ln1_g[l].reshape(1, d_model), ln1_b[l].reshape(1, d_model)
        l2g, l2b = ln2_g[l].reshape(1, d_model), ln2_b[l].reshape(1, d_model)

        bias_p = jnp.repeat(gmlp_bs[l].T, gc, axis=1)
        q, kt, kf_all, vx, vf_all, gm, su, gates = _in_proj(
            xp, w_in_b, cos_p, sin_p, bgate, ln_gg, ln_gb, ws_tril.astype(BF), bias_p, kv_stacked,
            wb=wbr, hd=hd, n_heads=n_heads, tm=tm_p, tk=tk, rows_per_seq=lp, prompt=True,
            layer=l, depth=depth)
        kv_stacked = (kf_all, vf_all)
        attn = _attn_prompt(lam, q, kt, vx, an_g, n_seq=bp, seq=lp, n_heads=n_heads, hd=hd, tk=tk,
                            post_scale=post_scale)
        ssm, hre, him = _ssm_prompt(su.reshape(bp, lp, wbr), bw, cw, pq, d_row, glu_b16, glu_bias,
                                    t_chunk=t_chunk)
        x1 = _merge(attn, gm, ssm.reshape(rp, wbr), gates, xp, wb_b, wo_b, l1g, l1b,
                    tm=min(512, rp), alpha=alpha, layer=l)
        xp = _ffn(x1, wfi_b, wfo_b, l2g, l2b, tm=min(512, rp), tf=tf, alpha=alpha, layer=l)
        outs["hrp"].append(hre[SUBLANES // 2:].reshape(bp, n_grp_s, n_state))
        outs["hip"].append(him[SUBLANES // 2:].reshape(bp, n_grp_s, n_state))

        ws_s = jnp.einsum('gts,bc->gtbsc', ws_tril[:, :ls, :ls], jnp.eye(bs, dtype=F32)).reshape(n_grp_g, rs, rs)
        bias_s = jnp.repeat(jnp.repeat(gmlp_bs[l][:, :ls].T, bs, axis=0), gc, axis=1)
        q, kb, kf, vb, vf, gm, vn, su, gates = _in_proj(
            xs, w_in_b, cos_s, sin_s, bgate, ln_gg, ln_gb, ws_s.astype(BF), bias_s,
            wb=wbr, hd=hd, n_heads=n_heads, tm=rs, tk=rs, rows_per_seq=rs, prompt=False, layer=l)
        q3 = q.reshape(ls, bs, wbr).transpose(1, 0, 2)
        zq = jnp.zeros((bs, SUBLANES - ls, wbr), BF)
        q16 = jnp.concatenate([q3, zq, q3, zq], axis=1).reshape(bs * 2 * SUBLANES, wbr)
        pad_new = lambda a: jnp.pad(a.reshape(ls, bs, wbr).transpose(1, 0, 2),
                                    ((0, 0), (0, page - ls), (0, 0))).reshape(bs * page, wbr)
        attn8 = _attn_sample(pt_flat, lam, q16, pad_new(kb), pad_new(vb), an_g, ck, cv,
                             layer=l, n_seq=bs, n_pages=n_pages, n_heads=n_heads, hd=hd,
                             post_scale=post_scale)
        attn = attn8.reshape(bs, SUBLANES, wbr)[:, :ls].transpose(1, 0, 2).reshape(rs, wbr).astype(BF)
        ssm, hre, him = _ssm_sample(su, state_ssm_re[l].reshape(bs, ns), state_ssm_im[l].reshape(bs, ns),
                                    bw, cw, a_flat, d_row, glu_b16, glu_bias, n_steps=ls)
        x1 = _merge(attn, gm, ssm, gates, xs, wb_b, wo_b, l1g, l1b, tm=rs, alpha=alpha, layer=l)
        xs = _ffn(x1, wfi_b, wfo_b, l2g, l2b, tm=rs, tf=tf, alpha=alpha, layer=l)
        to_seq_major = lambda a: a.reshape(ls, bs, -1).transpose(1, 0, 2)
        outs["ks"].append(to_seq_major(kf).reshape(bs, ls, n_heads, dv))
        outs["vs"].append(to_seq_major(vf).reshape(bs, ls, n_heads, dv))
        outs["hrs"].append(hre.reshape(bs, n_grp_s, n_state))
        outs["his"].append(him.reshape(bs, n_grp_s, n_state))
        outs["gvs"].append(to_seq_major(vn))

    st = lambda k: jnp.stack(outs[k])
    kv_shape = (depth, bp, lp // page, page, n_heads, dv)
    return (xp.reshape(bp, lp, d_model), xs.reshape(ls, bs, d_model).transpose(1, 0, 2),
            kv_stacked[0].reshape(kv_shape), kv_stacked[1].reshape(kv_shape), st("ks"), st("vs"),
            st("hrp"), st("hip"), st("hrs"), st("his"), st("gvs"))
```

```python
import functools
import math

import jax
import jax.numpy as jnp
from jax import lax
from jax.experimental import pallas as pl
from jax.experimental.pallas import tpu as pltpu

BF = jnp.bfloat16
F32 = jnp.float32

LN_EPS = 1e-5
ROPE_THETA = 10000.0
NEG_BIG = -1e30
LOG2E = math.log2(math.e)
LANES = 128
SUBLANES = 8
MXU_K = 256
VMEM_LIMIT = 56 * 1024 * 1024


def _params(*sem):
    return pltpu.CompilerParams(dimension_semantics=sem, vmem_limit_bytes=VMEM_LIMIT)


def _layer_block(stacked, layer, **kw):
    zeros = (0,) * (stacked.ndim - 1)
    return pl.BlockSpec((None,) + stacked.shape[1:], lambda i: (layer,) + zeros, **kw)


def _layer_norm(r, g, b):
    mu = jnp.mean(r, axis=-1, keepdims=True)
    d = r - mu
    var = jnp.mean(d * d, axis=-1, keepdims=True)
    return d * lax.rsqrt(var + LN_EPS) * g + b


def _in_proj_kernel(*refs, hd, n_heads, q_scale, prompt, tk, n_alias):
    x_ref, w_ref, cos_ref, sin_ref, bg_ref, lng_ref, lnb_ref, ws_ref, bias_ref = refs[:9]
    rest = refs[9 + n_alias:]
    if prompt:
        q_ref, kt_ref, kf_ref, vx_ref, vf_ref, gm_ref, su_ref, gates_ref = rest
    else:
        q_ref, kb_ref, kf_ref, vb_ref, vf_ref, gm_ref, vn_ref, su_ref, gates_ref = rest
    tm = x_ref.shape[0]
    wb = q_ref.shape[1]
    dv = 2 * hd
    half = hd // 2
    xb = x_ref[...].astype(BF)

    def col(c):
        return jnp.dot(xb, w_ref[:, c * wb:(c + 1) * wb], preferred_element_type=F32)

    def rope(a):
        cos = cos_ref[...]
        sin = sin_ref[...]
        outs = []
        for c in range(a.shape[1] // LANES):
            blk = a[:, c * LANES:(c + 1) * LANES]
            lane = lax.broadcasted_iota(jnp.int32, blk.shape, 1)
            swapped = jnp.where((lane % hd) < half,
                                pltpu.roll(blk, LANES - half, 1), pltpu.roll(blk, half, 1))
            outs.append(blk * cos + swapped * sin)
        return jnp.concatenate(outs, axis=1)

    def store_head_rows(dst_ref, a):
        for h in range(n_heads):
            dst_ref[pl.ds(h, tm, stride=n_heads), :] = a[:, h * dv:(h + 1) * dv]

    q_ref[...] = (rope(col(0)) * q_scale).astype(BF)

    k = rope(col(1))
    if prompt:
        store_head_rows(kf_ref, k)
        for t in range(tm // tk):
            kt_ref[t] = k[t * tk:(t + 1) * tk, :].T.astype(BF)
    else:
        kf_ref[...] = k
        kb_ref[...] = k.astype(BF)

    v = col(2)
    if prompt:
        store_head_rows(vf_ref, v)
        ones = jnp.ones((tm, dv), BF)
        for h in range(n_heads):
            vx_ref[:, 2 * h * dv:(2 * h + 1) * dv] = v[:, h * dv:(h + 1) * dv].astype(BF)
            vx_ref[:, (2 * h + 1) * dv:(2 * h + 2) * dv] = ones
    else:
        vf_ref[...] = v
        vb_ref[...] = v.astype(BF)

    vn = _layer_norm(jax.nn.gelu(col(4)), lng_ref[...], lnb_ref[...])
    if not prompt:
        vn_ref[...] = vn
    vb16 = vn.astype(BF)
    u = jax.nn.gelu(col(3))
    n_grp, chunk = ws_ref.shape[0], ws_ref.shape[1]
    gc = wb // n_grp
    for n in range(tm // chunk):
        rs = slice(n * chunk, (n + 1) * chunk)
        for g in range(n_grp):
            cs = slice(g * gc, (g + 1) * gc)
            mixed = jnp.dot(ws_ref[g], vb16[rs, cs], preferred_element_type=F32) + bias_ref[:, cs]
            gm_ref[rs, cs] = (u[rs, cs] * mixed).astype(BF)

    su_ref[...] = col(5).astype(BF)

    for n in range(gates_ref.shape[1] // wb):
        cs = slice(n * wb, (n + 1) * wb)
        gates_ref[:, cs] = jax.nn.sigmoid(col(6 + n) + bg_ref[:, cs]).astype(BF)


def _in_proj(x, w_b, cos_t, sin_t, b_gate, ln_g, ln_b, ws_b, bias_full, kv_prev=(), *,
             wb, hd, n_heads, tm, tk, rows_per_seq, prompt, layer=0, depth=1):
    R, D = x.shape
    n_gate = w_b.shape[2] // wb - 6
    nt = rows_per_seq // tm
    dv = 2 * hd
    chunk = ws_b.shape[1]
    row_map = lambda i: (i, 0)
    const2 = lambda i: (0, 0)
    tab_map = lambda i: (i % nt, 0)
    blk = lambda n=1: pl.BlockSpec((tm, n * wb), row_map)
    sds = jax.ShapeDtypeStruct
    if prompt:
        stacked = sds((depth, R * n_heads, dv), F32)
        stacked_spec = pl.BlockSpec((None, tm * n_heads, dv), lambda i: (layer, i, 0))
        out_shape = (sds((R, wb), BF), sds((R // tk, wb, tk), BF), stacked, sds((R, 2 * wb), BF), stacked,
                     sds((R, wb), BF), sds((R, wb), BF), sds((R, n_gate * wb), BF))
        out_specs = (blk(), pl.BlockSpec((tm // tk, wb, tk), lambda i: (i, 0, 0)), stacked_spec,
                     blk(2), stacked_spec, blk(), blk(), blk(n_gate))
        aliases = {9 + n: (2, 4)[n] for n in range(len(kv_prev))}
    else:
        out_shape = (sds((R, wb), BF), sds((R, wb), BF), sds((R, wb), F32), sds((R, wb), BF), sds((R, wb), F32),
                     sds((R, wb), BF), sds((R, wb), F32), sds((R, wb), BF), sds((R, n_gate * wb), BF))
        out_specs = (blk(),) * 8 + (blk(n_gate),)
        aliases = {}
    return pl.pallas_call(
        functools.partial(_in_proj_kernel, hd=hd, n_heads=n_heads, q_scale=hd ** -0.5 * LOG2E,
                          prompt=prompt, tk=tk, n_alias=len(kv_prev)),
        grid=(R // tm,),
        in_specs=[pl.BlockSpec((tm, D), row_map),
                  _layer_block(w_b, layer, pipeline_mode=pl.Buffered(1)),
                  pl.BlockSpec((tm, LANES), tab_map),
                  pl.BlockSpec((tm, LANES), tab_map),
                  pl.BlockSpec((1, n_gate * wb), const2),
                  pl.BlockSpec((1, wb), const2), pl.BlockSpec((1, wb), const2),
                  pl.BlockSpec(ws_b.shape, lambda i: (0, 0, 0)),
                  pl.BlockSpec((chunk, wb), const2)]
                 + [pl.BlockSpec(memory_space=pl.ANY)] * len(kv_prev),
        out_specs=out_specs,
        out_shape=out_shape,
        input_output_aliases=aliases,
        compiler_params=_params("parallel"),
        name="in_proj_prompt" if prompt else "in_proj_sample",
    )(x, w_b, cos_t, sin_t, b_gate, ln_g, ln_b, ws_b, bias_full, *kv_prev)


def _attn_prompt_kernel(lam_ref, q_ref, kt_ref, vx_ref, g_ref, o_ref, qs_ref, m_ref, acc_ref,
                        *, tq, tk, hd, n_heads, post_scale, unroll):
    qi = pl.program_id(1)
    dv = 2 * hd
    for h in range(n_heads):
        q = q_ref[:, h * dv:(h + 1) * dv].astype(F32)
        lane = lax.broadcasted_iota(jnp.int32, q.shape, 1)
        qs_ref[h, 0:tq, :] = jnp.where(lane < hd, q, 0.0).astype(BF)
        qs_ref[h, tq:2 * tq, :] = jnp.where(lane >= hd, q, 0.0).astype(BF)
    m_ref[...] = jnp.full(m_ref.shape, -jnp.inf, F32)
    acc_ref[...] = jnp.zeros(acc_ref.shape, F32)

    def tile(ki, masked):
        k0 = pl.multiple_of(ki * tk, tk)
        for h in range(n_heads):
            s = jnp.dot(qs_ref[h], kt_ref[ki, h * dv:(h + 1) * dv, :], preferred_element_type=F32)
            if masked:
                row = lax.broadcasted_iota(jnp.int32, s.shape, 0)
                col = lax.broadcasted_iota(jnp.int32, s.shape, 1)
                s = jnp.where(ki * tk + col <= qi * tq + row % tq, s, NEG_BIG)
            m_prev = m_ref[h]
            m_new = jnp.maximum(m_prev, jnp.max(s, axis=1, keepdims=True))
            alpha = jnp.exp2(m_prev - m_new)
            p = jnp.exp2(s - m_new[:, :1]).astype(BF)
            pv = jnp.dot(p, vx_ref[pl.ds(k0, tk), 2 * h * dv:(2 * h + 2) * dv], preferred_element_type=F32)
            acc_ref[h, :, 0:dv] = alpha * acc_ref[h, :, 0:dv] + pv[:, 0:dv]
            acc_ref[h, :, dv:2 * dv] = alpha * acc_ref[h, :, dv:2 * dv] + pv[:, dv:2 * dv]
            m_ref[h] = m_new

    n_full = (qi * tq) // tk
    n_groups = n_full // unroll
    tail = n_full + 1 - n_groups * unroll

    def full_group(kg, carry):
        for u in range(unroll):
            tile(unroll * kg + u, False)
        return carry

    lax.fori_loop(0, n_groups, full_group, 0)
    for r in range(1, unroll + 1):
        @pl.when(tail == r)
        def _(r=r):
            for u in range(r - 1):
                tile(n_groups * unroll + u, False)
            tile(n_groups * unroll + r - 1, True)

    for h in range(n_heads):
        t = acc_ref[h, :, 0:dv] / acc_ref[h, :, dv:2 * dv]
        o = t[0:tq] - lam_ref[0] * t[tq:2 * tq]
        ms = jnp.mean(o * o, axis=1, keepdims=True)
        o_ref[:, h * dv:(h + 1) * dv] = (o * lax.rsqrt(ms + LN_EPS) * g_ref[...] * post_scale).astype(BF)


def _attn_prompt(lam, q, kt, vx, g, *, n_seq, seq, n_heads, hd, tk, post_scale):
    tq = min(512, seq)
    dv = 2 * hd
    w = n_heads * dv
    nq = seq // tq
    return pl.pallas_call(
        functools.partial(_attn_prompt_kernel, tq=tq, tk=tk, hd=hd, n_heads=n_heads, post_scale=post_scale,
                          unroll=2),
        grid=(n_seq, nq),
        in_specs=[pl.BlockSpec(memory_space=pltpu.SMEM),
                  pl.BlockSpec((tq, w), lambda b, i: (b * nq + i, 0)),
                  pl.BlockSpec((seq // tk, w, tk), lambda b, i: (b, 0, 0)),
                  pl.BlockSpec((seq, 2 * w), lambda b, i: (b, 0)),
                  pl.BlockSpec((1, dv), lambda b, i: (0, 0))],
        out_specs=pl.BlockSpec((tq, w), lambda b, i: (b * nq + i, 0)),
        out_shape=jax.ShapeDtypeStruct(q.shape, BF),
        scratch_shapes=[pltpu.VMEM((n_heads, 2 * tq, dv), BF), pltpu.VMEM((n_heads, 2 * tq, dv), F32),
                        pltpu.VMEM((n_heads, 2 * tq, 2 * dv), F32)],
        compiler_params=_params("parallel", "arbitrary"),
        name="attn_prompt",
    )(lam, q, kt, vx, g)


def _attn_sample_kernel(pt_ref, lam_ref, q_ref, kn_ref, vn_ref, g_ref, *rest,
                        n_pg, n_heads, hd, post_scale):
    k_refs = rest[:n_pg]
    v_refs = rest[n_pg:2 * n_pg]
    o_ref, qp_ref, m_ref, l_ref, acc_ref = rest[2 * n_pg:]
    j = pl.program_id(1)
    dv = 2 * hd
    nr = q_ref.shape[0]
    page = k_refs[0].shape[0] // n_heads
    n_pairs = n_heads // 2
    nt_dims = (((1,), (1,)), ((), ()))

    @pl.when(j == 0)
    def _():
        for a in range(n_pairs):
            for side in range(2):
                h = 2 * a + side
                qh = q_ref[:, h * dv:(h + 1) * dv].astype(F32)
                row = lax.broadcasted_iota(jnp.int32, qh.shape, 0)
                lane = lax.broadcasted_iota(jnp.int32, qh.shape, 1)
                keep = (row < nr // 2) == (lane < hd)
                qm = jnp.where(keep, qh, 0.0).astype(BF)
                zero = jnp.zeros_like(qm)
                qp_ref[a, side * nr:(side + 1) * nr, :] = jnp.concatenate(
                    [qm, zero] if side == 0 else [zero, qm], axis=1)
        m_ref[...] = jnp.full(m_ref.shape, -jnp.inf, F32)
        l_ref[...] = jnp.zeros(l_ref.shape, F32)
        acc_ref[...] = jnp.zeros(acc_ref.shape, F32)

    def update(a, s, pv):
        m_prev = m_ref[a]
        m_new = jnp.maximum(m_prev, jnp.max(s, axis=1, keepdims=True))
        alpha = jnp.exp2(m_prev - m_new)
        p = jnp.exp2(s - m_new[:, :1])
        l_ref[a] = alpha * l_ref[a] + jnp.sum(p, axis=1, keepdims=True)
        acc_ref[a] = jnp.concatenate([alpha, alpha], axis=1) * acc_ref[a] + pv(p.astype(BF))
        m_ref[a] = m_new

    def pair_rows(page_refs, a):
        def head(h):
            return jnp.concatenate([r[pl.ds(h, page, stride=n_heads), :] for r in page_refs], axis=0)
        return jnp.concatenate([head(2 * a), head(2 * a + 1)], axis=1).astype(BF)

    for a in range(n_pairs):
        s = lax.dot_general(qp_ref[a], pair_rows(k_refs, a), nt_dims, preferred_element_type=F32)
        update(a, s, lambda p, a=a: jnp.dot(p, pair_rows(v_refs, a), preferred_element_type=F32))

    @pl.when(j == pl.num_programs(1) - 1)
    def _():
        for a in range(n_pairs):
            ps = slice(2 * a * dv, (2 * a + 2) * dv)
            s = lax.dot_general(qp_ref[a], kn_ref[:, ps], nt_dims, preferred_element_type=F32)
            row = lax.broadcasted_iota(jnp.int32, s.shape, 0)
            col = lax.broadcasted_iota(jnp.int32, s.shape, 1)
            s = jnp.where(col <= row % (nr // 2), s, NEG_BIG)
            update(a, s, lambda p, ps=ps: jnp.dot(p, vn_ref[:, ps], preferred_element_type=F32))
            for side in range(2):
                h = 2 * a + side
                t = (acc_ref[a, side * nr:(side + 1) * nr, side * dv:(side + 1) * dv]
                     / l_ref[a, side * nr:(side + 1) * nr, :])
                o = t[0:nr // 2] - lam_ref[0] * t[nr // 2:nr]
                ms = jnp.mean(o * o, axis=1, keepdims=True)
                o_ref[:, h * dv:(h + 1) * dv] = o * lax.rsqrt(ms + LN_EPS) * g_ref[...] * post_scale


def _attn_sample(page_table_flat, lam, q16, kn, vn, g, cache_k, cache_v, *,
                 layer, n_seq, n_pages, n_heads, hd, post_scale):
    n_pg = math.gcd(32, n_pages)
    nr = 2 * SUBLANES
    dv = 2 * hd
    w = n_heads * dv
    page_rows = cache_k.shape[2]
    page = page_rows // n_heads

    def kv_spec(g):
        return pl.BlockSpec((None, None, page_rows, dv),
                            lambda b, j, pt: (layer, pt[b * n_pages + j * n_pg + g], 0, 0))

    grid_spec = pltpu.PrefetchScalarGridSpec(
        num_scalar_prefetch=1,
        grid=(n_seq, n_pages // n_pg),
        in_specs=[pl.BlockSpec(memory_space=pltpu.SMEM),
                  pl.BlockSpec((nr, w), lambda b, j, pt: (b, 0)),
                  pl.BlockSpec((page, w), lambda b, j, pt: (b, 0)),
                  pl.BlockSpec((page, w), lambda b, j, pt: (b, 0)),
                  pl.BlockSpec((1, dv), lambda b, j, pt: (0, 0))]
                 + [kv_spec(g) for g in range(n_pg)] + [kv_spec(g) for g in range(n_pg)],
        out_specs=pl.BlockSpec((nr // 2, w), lambda b, j, pt: (b, 0)),
        scratch_shapes=[pltpu.VMEM((n_heads // 2, 2 * nr, 2 * dv), BF),
                        pltpu.VMEM((n_heads // 2, 2 * nr, dv), F32), pltpu.VMEM((n_heads // 2, 2 * nr, dv), F32),
                        pltpu.VMEM((n_heads // 2, 2 * nr, 2 * dv), F32)])
    return pl.pallas_call(
        functools.partial(_attn_sample_kernel, n_pg=n_pg, n_heads=n_heads, hd=hd, post_scale=post_scale),
        grid_spec=grid_spec,
        out_shape=jax.ShapeDtypeStruct((n_seq * (nr // 2), w), F32),
        compiler_params=_params("parallel", "arbitrary"),
        name="attn_sample",
    )(page_table_flat, lam, q16, kn, vn, g, *([cache_k] * n_pg), *([cache_v] * n_pg))


def _ssm_input_proj(ub, bw_ref, xre_ref, xim_ref):
    n_slab = bw_ref.shape[0]
    ns = xre_ref.shape[1] // n_slab
    for h in range(n_slab):
        uh = ub[:, h * MXU_K:(h + 1) * MXU_K]
        xre_ref[:, h * ns:(h + 1) * ns] = jnp.dot(uh, bw_ref[h, :, 0:ns], preferred_element_type=F32)
        xim_ref[:, h * ns:(h + 1) * ns] = jnp.dot(uh, bw_ref[h, :, ns:2 * ns], preferred_element_type=F32)


def _ssm_output(u, hre_ref, him_ref, cw_ref, d_ref, gw_ref, gb_ref):
    n_slab = cw_ref.shape[1]
    ns = hre_ref.shape[1] // n_slab
    ys = []
    for h in range(n_slab):
        ss = slice(h * ns, (h + 1) * ns)
        ys.append(jnp.dot(hre_ref[:, ss].astype(BF), cw_ref[0, h], preferred_element_type=F32)
                  + jnp.dot(him_ref[:, ss].astype(BF), cw_ref[1, h], preferred_element_type=F32))
    y = jnp.concatenate(ys, axis=1) + d_ref[...] * u
    a = jax.nn.gelu(y)
    z = jnp.dot(a.astype(BF), gw_ref[...], preferred_element_type=F32) + gb_ref[...]
    return a * jax.nn.sigmoid(z)


def _ssm_prompt_kernel(u_ref, bw_ref, cw_ref, pq_ref, d_ref, gw_ref, gb_ref,
                       out_ref, hre_out, him_out, utb_ref, xre_ref, xim_ref, cre_ref, cim_ref, *, lane_grp):
    @pl.when(pl.program_id(0) == 0)
    def _():
        cre_ref[...] = jnp.zeros(cre_ref.shape, F32)
        cim_ref[...] = jnp.zeros(cim_ref.shape, F32)

    n_seq, t_chunk = u_ref.shape[0], u_ref.shape[1]
    n_slab = utb_ref.shape[0]
    for b in range(n_seq):
        ub = u_ref[b].astype(F32)
        for c in range(n_slab):
            utb_ref[c, pl.ds(b, t_chunk, stride=n_seq), :] = ub[:, c * LANES:(c + 1) * LANES]
    u = jnp.concatenate([utb_ref[c] for c in range(n_slab)], axis=1)
    ub16 = u.astype(BF)

    rows, ns = xre_ref.shape
    half = SUBLANES // 2
    ns_slab = ns // bw_ref.shape[0]
    n_lg = ns // lane_grp

    def input_proj(lg):
        slab, off = divmod(lg * lane_grp, ns_slab)
        uh = ub16[:, slab * MXU_K:(slab + 1) * MXU_K]
        sl = slice(lg * lane_grp, (lg + 1) * lane_grp)
        xre_ref[:, sl] = jnp.dot(uh, bw_ref[slab, :, off:off + lane_grp], preferred_element_type=F32)
        xim_ref[:, sl] = jnp.dot(uh, bw_ref[slab, :, ns_slab + off:ns_slab + off + lane_grp],
                                 preferred_element_type=F32)

    input_proj(0)
    for lg in range(n_lg):
        if lg + 1 < n_lg:
            input_proj(lg + 1)
        sl = slice(lg * lane_grp, (lg + 1) * lane_grp)
        p_re, p_im, q_re, q_im = pq_ref[0, :, sl], pq_ref[1, :, sl], pq_ref[2, :, sl], pq_ref[3, :, sl]
        upper = lax.broadcasted_iota(jnp.int32, (SUBLANES, lane_grp), 0) >= half
        c_re, c_im = cre_ref[:, sl], cim_ref[:, sl]
        for i in range(rows // SUBLANES):
            rs = slice(i * SUBLANES, (i + 1) * SUBLANES)
            x_re = xre_ref[rs, sl]
            x_im = xim_ref[rs, sl]
            s_re = pltpu.roll(x_re, half, 0)
            s_im = pltpu.roll(x_im, half, 0)
            h_re = x_re + q_re * s_re - q_im * s_im + p_re * c_re - p_im * c_im
            h_im = x_im + q_re * s_im + q_im * s_re + p_re * c_im + p_im * c_re
            xre_ref[rs, sl] = h_re
            xim_ref[rs, sl] = h_im
            c_re = jnp.where(upper, h_re, pltpu.roll(h_re, half, 0))
            c_im = jnp.where(upper, h_im, pltpu.roll(h_im, half, 0))
        cre_ref[:, sl] = c_re
        cim_ref[:, sl] = c_im

    y = _ssm_output(u, xre_ref, xim_ref, cw_ref, d_ref, gw_ref, gb_ref)
    for c in range(n_slab):
        utb_ref[c] = y[:, c * LANES:(c + 1) * LANES]
    for b in range(n_seq):
        for c in range(n_slab):
            out_ref[b, :, c * LANES:(c + 1) * LANES] = utb_ref[c, pl.ds(b, t_chunk, stride=n_seq), :].astype(BF)
    hre_out[...] = cre_ref[...]
    him_out[...] = cim_ref[...]


def _ssm_prompt(su, bw, cw, pq, d, gw, gb, *, t_chunk):
    n_seq, seq, w = su.shape
    ns = pq.shape[2]
    rows = n_seq * t_chunk
    blk_map = lambda i: (0, i, 0)
    const2 = lambda i: (0, 0)
    return pl.pallas_call(
        functools.partial(_ssm_prompt_kernel, lane_grp=min(512, ns)),
        grid=(seq // t_chunk,),
        in_specs=[pl.BlockSpec((n_seq, t_chunk, w), blk_map),
                  pl.BlockSpec(bw.shape, lambda i: (0, 0, 0)),
                  pl.BlockSpec(cw.shape, lambda i: (0, 0, 0, 0)),
                  pl.BlockSpec(pq.shape, lambda i: (0, 0, 0)),
                  pl.BlockSpec((1, w), const2),
                  pl.BlockSpec(gw.shape, const2),
                  pl.BlockSpec((1, w), const2)],
        out_specs=(pl.BlockSpec((n_seq, t_chunk, w), blk_map),
                   pl.BlockSpec((SUBLANES, ns), const2), pl.BlockSpec((SUBLANES, ns), const2)),
        out_shape=(jax.ShapeDtypeStruct((n_seq, seq, w), BF),
                   jax.ShapeDtypeStruct((SUBLANES, ns), F32), jax.ShapeDtypeStruct((SUBLANES, ns), F32)),
        scratch_shapes=[pltpu.VMEM((w // LANES, rows, LANES), F32),
                        pltpu.VMEM((rows, ns), F32), pltpu.VMEM((rows, ns), F32),
                        pltpu.VMEM((SUBLANES, ns), F32), pltpu.VMEM((SUBLANES, ns), F32)],
        compiler_params=_params("arbitrary"),
        name="ssm_prompt",
    )(su, bw, cw, pq, d, gw, gb)


def _ssm_sample_kernel(u_ref, h0re_ref, h0im_ref, bw_ref, cw_ref, a_ref, d_ref, gw_ref, gb_ref,
                       out_ref, hre_out, him_out, xre_ref, xim_ref, *, n_steps):
    ub = u_ref[...]
    _ssm_input_proj(ub, bw_ref, xre_ref, xim_ref)
    nb = h0re_ref.shape[0]
    a_re, a_im = a_ref[0], a_ref[1]
    h_re, h_im = h0re_ref[...], h0im_ref[...]
    for t in range(n_steps):
        rs = slice(t * nb, (t + 1) * nb)
        h_re, h_im = (a_re * h_re - a_im * h_im + xre_ref[rs, :],
                      a_re * h_im + a_im * h_re + xim_ref[rs, :])
        xre_ref[rs, :] = h_re
        xim_ref[rs, :] = h_im
    out_ref[...] = _ssm_output(ub.astype(F32), xre_ref, xim_ref, cw_ref, d_ref, gw_ref, gb_ref).astype(BF)
    hre_out[...] = h_re
    him_out[...] = h_im


def _ssm_sample(su, h0_re, h0_im, bw, cw, a, d, gw, gb, *, n_steps):
    rows, w = su.shape
    nb, ns = h0_re.shape
    full = lambda arr: pl.BlockSpec(arr.shape, lambda i, n=arr.ndim: (0,) * n)
    return pl.pallas_call(
        functools.partial(_ssm_sample_kernel, n_steps=n_steps),
        grid=(1,),
        in_specs=[full(su), full(h0_re), full(h0_im), full(bw), full(cw), full(a), full(d), full(gw), full(gb)],
        out_specs=(pl.BlockSpec((rows, w), lambda i: (0, 0)),
                   pl.BlockSpec((nb, ns), lambda i: (0, 0)), pl.BlockSpec((nb, ns), lambda i: (0, 0))),
        out_shape=(jax.ShapeDtypeStruct((rows, w), BF),
                   jax.ShapeDtypeStruct((nb, ns), F32), jax.ShapeDtypeStruct((nb, ns), F32)),
        scratch_shapes=[pltpu.VMEM((rows, ns), F32), pltpu.VMEM((rows, ns), F32)],
        compiler_params=_params("arbitrary"),
        name="ssm_sample",
    )(su, h0_re, h0_im, bw, cw, a, d, gw, gb)


def _merge_kernel(at_ref, gm_ref, ss_ref, gates_ref, x_ref, wb_ref, wo_ref, lg_ref, lb_ref, o_ref, *, alpha):
    d = x_ref.shape[1]
    merged = None
    for n, br in enumerate((at_ref, gm_ref, ss_ref)):
        proj = jnp.dot(br[...], wb_ref[n], preferred_element_type=F32)
        gated = gates_ref[:, n * d:(n + 1) * d].astype(F32) * proj
        merged = gated if merged is None else merged + gated
    mix = jnp.dot(merged.astype(BF), wo_ref[...], preferred_element_type=F32)
    o_ref[...] = _layer_norm(alpha * x_ref[...] + mix, lg_ref[...], lb_ref[...])


def _merge(attn, gm, ssm, gates, x, wb_b, wo_b, ln_g, ln_b, *, tm, alpha, layer):
    R, d = x.shape
    w = attn.shape[1]
    row_map = lambda i: (i, 0)
    const2 = lambda i: (0, 0)
    return pl.pallas_call(
        functools.partial(_merge_kernel, alpha=alpha),
        grid=(R // tm,),
        in_specs=[pl.BlockSpec((tm, w), row_map), pl.BlockSpec((tm, w), row_map),
                  pl.BlockSpec((tm, w), row_map),
                  pl.BlockSpec((tm, gates.shape[1]), row_map),
                  pl.BlockSpec((tm, d), row_map),
                  _layer_block(wb_b, layer), _layer_block(wo_b, layer),
                  pl.BlockSpec((1, d), const2), pl.BlockSpec((1, d), const2)],
        out_specs=pl.BlockSpec((tm, d), row_map),
        out_shape=jax.ShapeDtypeStruct((R, d), F32),
        compiler_params=_params("parallel"),
        name="merge",
    )(attn, gm, ssm, gates, x, wb_b, wo_b, ln_g, ln_b)


def _ffn_kernel(x_ref, wi_ref, wo_ref, lg_ref, lb_ref, o_ref, hid_ref, *, alpha, tf):
    x = x_ref[...]
    xb = x.astype(BF)
    d_ff = wo_ref.shape[0]
    for c in range(d_ff // tf):
        gate = jnp.dot(xb, wi_ref[:, c * tf:(c + 1) * tf], preferred_element_type=F32)
        up = jnp.dot(xb, wi_ref[:, d_ff + c * tf:d_ff + (c + 1) * tf], preferred_element_type=F32)
        hid_ref[:, c * tf:(c + 1) * tf] = (gate * jax.nn.sigmoid(gate) * up).astype(BF)
    f = jnp.dot(hid_ref[...], wo_ref[...], preferred_element_type=F32)
    o_ref[...] = _layer_norm(alpha * x + f, lg_ref[...], lb_ref[...])


def _ffn(x, w_in_b, w_out_b, ln_g, ln_b, *, tm, tf, alpha, layer):
    R, d = x.shape
    d_ff = w_out_b.shape[1]
    row_map = lambda i: (i, 0)
    const2 = lambda i: (0, 0)
    return pl.pallas_call(
        functools.partial(_ffn_kernel, alpha=alpha, tf=tf),
        grid=(R // tm,),
        in_specs=[pl.BlockSpec((tm, d), row_map),
                  _layer_block(w_in_b, layer, pipeline_mode=pl.Buffered(1)),
                  _layer_block(w_out_b, layer, pipeline_mode=pl.Buffered(1)),
                  pl.BlockSpec((1, d), const2), pl.BlockSpec((1, d), const2)],
        out_specs=pl.BlockSpec((tm, d), row_map),
        out_shape=jax.ShapeDtypeStruct((R, d), F32),
        scratch_shapes=[pltpu.VMEM((tm, d_ff), BF)],
        compiler_params=_params("parallel"),
        name="ffn",
    )(x, w_in_b, w_out_b, ln_g, ln_b)


def _rope_tables(pos, hd):
    half = hd // 2
    inv = ROPE_THETA ** (-jnp.arange(half, dtype=F32) / half)
    ang = pos.astype(F32)[:, None] * inv[None, :]
    cos, sin = jnp.cos(ang), jnp.sin(ang)
    reps = LANES // hd
    return (jnp.tile(jnp.concatenate([cos, cos], axis=1), (1, reps)),
            jnp.tile(jnp.concatenate([-sin, sin], axis=1), (1, reps)))


def _ssm_discretize(a_re, a_im, log_dt, b_re, b_im):
    dt = jnp.exp(log_dt.astype(F32))[:, None]
    ar, ai = a_re.astype(F32), a_im.astype(F32)
    mag = jnp.exp(dt * ar)
    abar_re = mag * jnp.cos(dt * ai)
    abar_im = mag * jnp.sin(dt * ai)
    nr, ni = abar_re - 1.0, abar_im
    den = ar * ar + ai * ai
    c_re = (nr * ar + ni * ai) / den
    c_im = (ni * ar - nr * ai) / den
    br, bi = b_re.astype(F32), b_im.astype(F32)
    bbar_re = c_re[..., None] * br - c_im[..., None] * bi
    bbar_im = c_re[..., None] * bi + c_im[..., None] * br
    return abar_re, abar_im, bbar_re, bbar_im


def _ssm_weights(a_re, a_im, log_dt, b_re, b_im, c_re, c_im):
    n_grp, n_state, grp_ch = b_re.shape
    abar_re, abar_im, bbar_re, bbar_im = _ssm_discretize(a_re, a_im, log_dt, b_re, b_im)
    gps = MXU_K // grp_ch
    n_slab = n_grp // gps
    eye = jnp.eye(gps, dtype=F32)

    def in_blocks(bb):
        bb = bb.reshape(n_slab, gps, n_state, grp_ch)
        return jnp.einsum('sgpc,gk->sgckp', bb, eye).reshape(n_slab, gps * grp_ch, gps * n_state)

    def out_blocks(cc):
        cc = cc.reshape(n_slab, gps, grp_ch, n_state)
        return jnp.einsum('sgcp,gk->sgpkc', cc, eye).reshape(n_slab, gps * n_state, gps * grp_ch)

    bw = jnp.concatenate([in_blocks(bbar_re), in_blocks(bbar_im)], axis=2).astype(BF)
    cw = jnp.stack([out_blocks(c_re.astype(F32)), -out_blocks(c_im.astype(F32))]).astype(BF)
    a_flat = jnp.stack([abar_re.reshape(1, -1), abar_im.reshape(1, -1)])
    a2_re = abar_re * abar_re - abar_im * abar_im
    a2_im = 2.0 * abar_re * abar_im
    half = SUBLANES // 2

    def halves(lo, hi):
        return jnp.concatenate([jnp.broadcast_to(lo.reshape(1, -1), (half, lo.size)),
                                jnp.broadcast_to(hi.reshape(1, -1), (half, hi.size))], axis=0)

    zero = jnp.zeros_like(abar_re)
    pq = jnp.stack([halves(abar_re, a2_re), halves(abar_im, a2_im),
                    halves(zero, abar_re), halves(zero, abar_im)])
    return bw, cw, a_flat, pq


def _lambda_init(l):
    return 0.8 - 0.6 * math.exp(-0.3 * l)


def kernel(x_prompt, x_sample, cache_k, cache_v, page_table, state_ssm_re, state_ssm_im, w_in, b_gate, lambda_q1, lambda_k1, lambda_q2, lambda_k2, attn_norm_g, gmlp_ln_g, gmlp_ln_b, gmlp_ws, gmlp_bs, ssm_a_re, ssm_a_im, ssm_log_dt, ssm_b_re, ssm_b_im, ssm_c_re, ssm_c_im, ssm_d, ssm_glu_w, ssm_glu_b, w_branch, w_out, ln1_g, ln1_b, ln2_g, ln2_b, w_ffn_in, w_ffn_out):
    bp, lp, d_model = x_prompt.shape
    bs, ls, _ = x_sample.shape
    depth = w_in.shape[0]
    n_pool, page, n_heads, dv = cache_k.shape[1:]
    hd = dv // 2
    n_pages = page_table.shape[1]
    n_past = n_pages * page
    wbr = ssm_glu_w.shape[1]
    n_grp_g, chunk = gmlp_ws.shape[1], gmlp_ws.shape[2]
    n_grp_s, n_state, grp_ch = ssm_b_re.shape[1:]
    ns = n_grp_s * n_state
    d_ff = w_ffn_out.shape[1]
    alpha = (2 * depth) ** 0.25
    rp, rs = bp * lp, bs * ls
    assert bp == SUBLANES // 2 and n_heads * dv == wbr and wbr % MXU_K == 0 and LANES % hd == 0
    assert dv == LANES and ls <= SUBLANES and rs == chunk and lp % page == 0

    xp = x_prompt.reshape(rp, d_model)
    xs = x_sample.transpose(1, 0, 2).reshape(rs, d_model)
    cos_p, sin_p = _rope_tables(jnp.arange(lp), hd)
    cos_s, sin_s = _rope_tables(jnp.repeat(n_past + jnp.arange(ls), bs), hd)
    ck = cache_k.reshape(depth, n_pool, page * n_heads, dv)
    cv = cache_v.reshape(depth, n_pool, page * n_heads, dv)
    pt_flat = page_table.reshape(-1).astype(jnp.int32)

    tm_p = min(512, lp)
    tk = min(512, lp)
    tf = 256 if d_ff % 256 == 0 else d_ff
    t_chunk = min(256, lp)
    gc = wbr // n_grp_g

    w_in_b, wb_b, wo_b = w_in.astype(BF), w_branch.astype(BF), w_out.astype(BF)
    wfi_b, wfo_b = w_ffn_in.astype(BF), w_ffn_out.astype(BF)

    outs = {k: [] for k in ("ks", "vs", "hrp", "hip", "hrs", "his", "gvs")}
    kv_stacked = ()
    for l in range(depth):
        lam_init = _lambda_init(l)
        post_scale = 1.0 - lam_init
        lam = (jnp.exp(jnp.sum(lambda_q1[l].astype(F32) * lambda_k1[l].astype(F32)))
               - jnp.exp(jnp.sum(lambda_q2[l].astype(F32) * lambda_k2[l].astype(F32)))
               + lam_init).reshape(1)
        glu_b16 = ssm_glu_w[l].astype(BF)
        an_g = attn_norm_g[l].reshape(1, dv)
        ws_tril = jnp.tril(gmlp_ws[l])
        bw, cw, a_flat, pq = _ssm_weights(ssm_a_re[l], ssm_a_im[l], ssm_log_dt[l], ssm_b_re[l], ssm_b_im[l],
                                          ssm_c_re[l], ssm_c_im[l])
        d_row = ssm_d[l].reshape(1, wbr)
        glu_bias = ssm_glu_b[l].reshape(1, wbr)
        ln_gg, ln_gb = gmlp_ln_g[l].reshape(1, wbr), gmlp_ln_b[l].reshape(1, wbr)
        bgate = b_gate[l].reshape(1, -1)
        l1g, l1b = ln1_g[l].reshape(1, d_model), ln1_b[l].reshape(1, d_model)
        l2g, l2b = ln2_g[l].reshape(1, d_model), ln2_b[l].reshape(1, d_model)

        bias_p = jnp.repeat(gmlp_bs[l].T, gc, axis=1)
        q, kt, kf_all, vx, vf_all, gm, su, gates = _in_proj(
            xp, w_in_b, cos_p, sin_p, bgate, ln_gg, ln_gb, ws_tril.astype(BF), bias_p, kv_stacked,
            wb=wbr, hd=hd, n_heads=n_heads, tm=tm_p, tk=tk, rows_per_seq=lp, prompt=True,
            layer=l, depth=depth)
        kv_stacked = (kf_all, vf_all)
        attn = _attn_prompt(lam, q, kt, vx, an_g, n_seq=bp, seq=lp, n_heads=n_heads, hd=hd, tk=tk,
                            post_scale=post_scale)
        ssm, hre, him = _ssm_prompt(su.reshape(bp, lp, wbr), bw, cw, pq, d_row, glu_b16, glu_bias,
                                    t_chunk=t_chunk)
        x1 = _merge(attn, gm, ssm.reshape(rp, wbr), gates, xp, wb_b, wo_b, l1g, l1b,
                    tm=min(512, rp), alpha=alpha, layer=l)
        xp = _ffn(x1, wfi_b, wfo_b, l2g, l2b, tm=min(512, rp), tf=tf, alpha=alpha, layer=l)
        outs["hrp"].append(hre[SUBLANES // 2:].reshape(bp, n_grp_s, n_state))
        outs["hip"].append(him[SUBLANES // 2:].reshape(bp, n_grp_s, n_state))

        ws_s = jnp.einsum('gts,bc->gtbsc', ws_tril[:, :ls, :ls], jnp.eye(bs, dtype=F32)).reshape(n_grp_g, rs, rs)
        bias_s = jnp.repeat(jnp.repeat(gmlp_bs[l][:, :ls].T, bs, axis=0), gc, axis=1)
        q, kb, kf, vb, vf, gm, vn, su, gates = _in_proj(
            xs, w_in_b, cos_s, sin_s, bgate, ln_gg, ln_gb, ws_s.astype(BF), bias_s,
            wb=wbr, hd=hd, n_heads=n_heads, tm=rs, tk=rs, rows_per_seq=rs, prompt=False, layer=l)
        q3 = q.reshape(ls, bs, wbr).transpose(1, 0, 2)
        zq = jnp.zeros((bs, SUBLANES - ls, wbr), BF)
        q16 = jnp.concatenate([q3, zq, q3, zq], axis=1).reshape(bs * 2 * SUBLANES, wbr)
        pad_new = lambda a: jnp.pad(a.reshape(ls, bs, wbr).transpose(1, 0, 2),
                                    ((0, 0), (0, page - ls), (0, 0))).reshape(bs * page, wbr)
        attn8 = _attn_sample(pt_flat, lam, q16, pad_new(kb), pad_new(vb), an_g, ck, cv,
                             layer=l, n_seq=bs, n_pages=n_pages, n_heads=n_heads, hd=hd,
                             post_scale=post_scale)
        attn = attn8.reshape(bs, SUBLANES, wbr)[:, :ls].transpose(1, 0, 2).reshape(rs, wbr).astype(BF)
        ssm, hre, him = _ssm_sample(su, state_ssm_re[l].reshape(bs, ns), state_ssm_im[l].reshape(bs, ns),
                                    bw, cw, a_flat, d_row, glu_b16, glu_bias, n_steps=ls)
        x1 = _merge(attn, gm, ssm, gates, xs, wb_b, wo_b, l1g, l1b, tm=rs, alpha=alpha, layer=l)
        xs = _ffn(x1, wfi_b, wfo_b, l2g, l2b, tm=rs, tf=tf, alpha=alpha, layer=l)
        to_seq_major = lambda a: a.reshape(ls, bs, -1).transpose(1, 0, 2)
        outs["ks"].append(to_seq_major(kf).reshape(bs, ls, n_heads, dv))
        outs["vs"].append(to_seq_major(vf).reshape(bs, ls, n_heads, dv))
        outs["hrs"].append(hre.reshape(bs, n_grp_s, n_state))
        outs["his"].append(him.reshape(bs, n_grp_s, n_state))
        outs["gvs"].append(to_seq_major(vn))

    st = lambda k: jnp.stack(outs[k])
    kv_shape = (depth, bp, lp // page, page, n_heads, dv)
    return (xp.reshape(bp, lp, d_model), xs.reshape(ls, bs, d_model).transpose(1, 0, 2),
            kv_stacked[0].reshape(kv_shape), kv_stacked[1].reshape(kv_shape), st("ks"), st("vs"),
            st("hrp"), st("hip"), st("hrs"), st("his"), st("gvs"))
```

```python
import functools
import math

import jax
import jax.numpy as jnp
from jax import lax
from jax.experimental import pallas as pl
from jax.experimental.pallas import tpu as pltpu

BF = jnp.bfloat16
F32 = jnp.float32

LN_EPS = 1e-5
ROPE_THETA = 10000.0
NEG_BIG = -1e30
LOG2E = math.log2(math.e)
LANES = 128
SUBLANES = 8
MXU_K = 256
VMEM_LIMIT = 56 * 1024 * 1024


def _params(*sem):
    return pltpu.CompilerParams(dimension_semantics=sem, vmem_limit_bytes=VMEM_LIMIT)


def _layer_block(stacked, layer, **kw):
    zeros = (0,) * (stacked.ndim - 1)
    return pl.BlockSpec((None,) + stacked.shape[1:], lambda i: (layer,) + zeros, **kw)


def _layer_norm(r, g, b):
    mu = jnp.mean(r, axis=-1, keepdims=True)
    d = r - mu
    var = jnp.mean(d * d, axis=-1, keepdims=True)
    return d * lax.rsqrt(var + LN_EPS) * g + b


def _in_proj_kernel(*refs, hd, n_heads, q_scale, prompt, tk, n_alias):
    x_ref, w_ref, cos_ref, sin_ref, bg_ref, lng_ref, lnb_ref, ws_ref, bias_ref = refs[:9]
    rest = refs[9 + n_alias:]
    if prompt:
        q_ref, kt_ref, kf_ref, vx_ref, vf_ref, gm_ref, su_ref, gates_ref = rest
    else:
        q_ref, kb_ref, kf_ref, vb_ref, vf_ref, gm_ref, vn_ref, su_ref, gates_ref = rest
    tm = x_ref.shape[0]
    wb = q_ref.shape[1]
    dv = 2 * hd
    half = hd // 2
    xb = x_ref[...].astype(BF)

    def col(c):
        return jnp.dot(xb, w_ref[:, c * wb:(c + 1) * wb], preferred_element_type=F32)

    def rope(a):
        cos = cos_ref[...]
        sin = sin_ref[...]
        outs = []
        for c in range(a.shape[1] // LANES):
            blk = a[:, c * LANES:(c + 1) * LANES]
            lane = lax.broadcasted_iota(jnp.int32, blk.shape, 1)
            swapped = jnp.where((lane % hd) < half,
                                pltpu.roll(blk, LANES - half, 1), pltpu.roll(blk, half, 1))
            outs.append(blk * cos + swapped * sin)
        return jnp.concatenate(outs, axis=1)

    def store_head_rows(dst_ref, a):
        for h in range(n_heads):
            dst_ref[pl.ds(h, tm, stride=n_heads), :] = a[:, h * dv:(h + 1) * dv]

    q_ref[...] = (rope(col(0)) * q_scale).astype(BF)

    k = rope(col(1))
    if prompt:
        store_head_rows(kf_ref, k)
        for t in range(tm // tk):
            kt_ref[t] = k[t * tk:(t + 1) * tk, :].T.astype(BF)
    else:
        kf_ref[...] = k
        kb_ref[...] = k.astype(BF)

    v = col(2)
    if prompt:
        store_head_rows(vf_ref, v)
        ones = jnp.ones((tm, dv), BF)
        for h in range(n_heads):
            vx_ref[:, 2 * h * dv:(2 * h + 1) * dv] = v[:, h * dv:(h + 1) * dv].astype(BF)
            vx_ref[:, (2 * h + 1) * dv:(2 * h + 2) * dv] = ones
    else:
        vf_ref[...] = v
        vb_ref[...] = v.astype(BF)

    vn = _layer_norm(jax.nn.gelu(col(4)), lng_ref[...], lnb_ref[...])
    if not prompt:
        vn_ref[...] = vn
    vb16 = vn.astype(BF)
    u = jax.nn.gelu(col(3))
    n_grp, chunk = ws_ref.shape[0], ws_ref.shape[1]
    gc = wb // n_grp
    for n in range(tm // chunk):
        rs = slice(n * chunk, (n + 1) * chunk)
        for g in range(n_grp):
            cs = slice(g * gc, (g + 1) * gc)
            mixed = jnp.dot(ws_ref[g], vb16[rs, cs], preferred_element_type=F32) + bias_ref[:, cs]
            gm_ref[rs, cs] = (u[rs, cs] * mixed).astype(BF)

    su_ref[...] = col(5).astype(BF)

    for n in range(gates_ref.shape[1] // wb):
        cs = slice(n * wb, (n + 1) * wb)
        gates_ref[:, cs] = jax.nn.sigmoid(col(6 + n) + bg_ref[:, cs]).astype(BF)


def _in_proj(x, w_b, cos_t, sin_t, b_gate, ln_g, ln_b, ws_b, bias_full, kv_prev=(), *,
             wb, hd, n_heads, tm, tk, rows_per_seq, prompt, layer=0, depth=1):
    R, D = x.shape
    n_gate = w_b.shape[2] // wb - 6
    nt = rows_per_seq // tm
    dv = 2 * hd
    chunk = ws_b.shape[1]
    row_map = lambda i: (i, 0)
    const2 = lambda i: (0, 0)
    tab_map = lambda i: (i % nt, 0)
    blk = lambda n=1: pl.BlockSpec((tm, n * wb), row_map)
    sds = jax.ShapeDtypeStruct
    if prompt:
        stacked = sds((depth, R * n_heads, dv), F32)
        stacked_spec = pl.BlockSpec((None, tm * n_heads, dv), lambda i: (layer, i, 0))
        out_shape = (sds((R, wb), BF), sds((R // tk, wb, tk), BF), stacked, sds((R, 2 * wb), BF), stacked,
                     sds((R, wb), BF), sds((R, wb), BF), sds((R, n_gate * wb), BF))
        out_specs = (blk(), pl.BlockSpec((tm // tk, wb, tk), lambda i: (i, 0, 0)), stacked_spec,
                     blk(2), stacked_spec, blk(), blk(), blk(n_gate))
        aliases = {9 + n: (2, 4)[n] for n in range(len(kv_prev))}
    else:
        out_shape = (sds((R, wb), BF), sds((R, wb), BF), sds((R, wb), F32), sds((R, wb), BF), sds((R, wb), F32),
                     sds((R, wb), BF), sds((R, wb), F32), sds((R, wb), BF), sds((R, n_gate * wb), BF))
        out_specs = (blk(),) * 8 + (blk(n_gate),)
        aliases = {}
    return pl.pallas_call(
        functools.partial(_in_proj_kernel, hd=hd, n_heads=n_heads, q_scale=hd ** -0.5 * LOG2E,
                          prompt=prompt, tk=tk, n_alias=len(kv_prev)),
        grid=(R // tm,),
        in_specs=[pl.BlockSpec((tm, D), row_map),
                  _layer_block(w_b, layer, pipeline_mode=pl.Buffered(1)),
                  pl.BlockSpec((tm, LANES), tab_map),
                  pl.BlockSpec((tm, LANES), tab_map),
                  pl.BlockSpec((1, n_gate * wb), const2),
                  pl.BlockSpec((1, wb), const2), pl.BlockSpec((1, wb), const2),
                  pl.BlockSpec(ws_b.shape, lambda i: (0, 0, 0)),
                  pl.BlockSpec((chunk, wb), const2)]
                 + [pl.BlockSpec(memory_space=pl.ANY)] * len(kv_prev),
        out_specs=out_specs,
        out_shape=out_shape,
        input_output_aliases=aliases,
        compiler_params=_params("parallel"),
        name="in_proj_prompt" if prompt else "in_proj_sample",
    )(x, w_b, cos_t, sin_t, b_gate, ln_g, ln_b, ws_b, bias_full, *kv_prev)


def _attn_prompt_kernel(lam_ref, q_ref, kt_ref, vx_ref, g_ref, o_ref, qs_ref, m_ref, acc_ref,
                        *, tq, tk, hd, n_heads, post_scale, unroll):
    qi = pl.program_id(1)
    dv = 2 * hd
    for h in range(n_heads):
        q = q_ref[:, h * dv:(h + 1) * dv].astype(F32)
        lane = lax.broadcasted_iota(jnp.int32, q.shape, 1)
        qs_ref[h, 0:tq, :] = jnp.where(lane < hd, q, 0.0).astype(BF)
        qs_ref[h, tq:2 * tq, :] = jnp.where(lane >= hd, q, 0.0).astype(BF)
    m_ref[...] = jnp.full(m_ref.shape, -jnp.inf, F32)
    acc_ref[...] = jnp.zeros(acc_ref.shape, F32)

    def tile(ki, masked):
        k0 = pl.multiple_of(ki * tk, tk)
        for h in range(n_heads):
            s = jnp.dot(qs_ref[h], kt_ref[ki, h * dv:(h + 1) * dv, :], preferred_element_type=F32)
            if masked:
                row = lax.broadcasted_iota(jnp.int32, s.shape, 0)
                col = lax.broadcasted_iota(jnp.int32, s.shape, 1)
                s = jnp.where(ki * tk + col <= qi * tq + row % tq, s, NEG_BIG)
            m_prev = m_ref[h]
            m_new = jnp.maximum(m_prev, jnp.max(s, axis=1, keepdims=True))
            alpha = jnp.exp2(m_prev - m_new)
            p = jnp.exp2(s - m_new[:, :1]).astype(BF)
            pv = jnp.dot(p, vx_ref[pl.ds(k0, tk), 2 * h * dv:(2 * h + 2) * dv], preferred_element_type=F32)
            acc_ref[h, :, 0:dv] = alpha * acc_ref[h, :, 0:dv] + pv[:, 0:dv]
            acc_ref[h, :, dv:2 * dv] = alpha * acc_ref[h, :, dv:2 * dv] + pv[:, dv:2 * dv]
            m_ref[h] = m_new

    n_full = (qi * tq) // tk
    n_groups = n_full // unroll
    tail = n_full + 1 - n_groups * unroll

    def full_group(kg, carry):
        for u in range(unroll):
            tile(unroll * kg + u, False)
        return carry

    lax.fori_loop(0, n_groups, full_group, 0)
    for r in range(1, unroll + 1):
        @pl.when(tail == r)
        def _(r=r):
            for u in range(r - 1):
                tile(n_groups * unroll + u, False)
            tile(n_groups * unroll + r - 1, True)

    for h in range(n_heads):
        t = acc_ref[h, :, 0:dv] / acc_ref[h, :, dv:2 * dv]
        o = t[0:tq] - lam_ref[0] * t[tq:2 * tq]
        ms = jnp.mean(o * o, axis=1, keepdims=True)
        o_ref[:, h * dv:(h + 1) * dv] = (o * lax.rsqrt(ms + LN_EPS) * g_ref[...] * post_scale).astype(BF)


def _attn_prompt(lam, q, kt, vx, g, *, n_seq, seq, n_heads, hd, tk, post_scale):
    tq = min(512, seq)
    dv = 2 * hd
    w = n_heads * dv
    nq = seq // tq
    return pl.pallas_call(
        functools.partial(_attn_prompt_kernel, tq=tq, tk=tk, hd=hd, n_heads=n_heads, post_scale=post_scale,
                          unroll=2),
        grid=(n_seq, nq),
        in_specs=[pl.BlockSpec(memory_space=pltpu.SMEM),
                  pl.BlockSpec((tq, w), lambda b, i: (b * nq + i, 0)),
                  pl.BlockSpec((seq // tk, w, tk), lambda b, i: (b, 0, 0)),
                  pl.BlockSpec((seq, 2 * w), lambda b, i: (b, 0)),
                  pl.BlockSpec((1, dv), lambda b, i: (0, 0))],
        out_specs=pl.BlockSpec((tq, w), lambda b, i: (b * nq + i, 0)),
        out_shape=jax.ShapeDtypeStruct(q.shape, BF),
        scratch_shapes=[pltpu.VMEM((n_heads, 2 * tq, dv), BF), pltpu.VMEM((n_heads, 2 * tq, dv), F32),
                        pltpu.VMEM((n_heads, 2 * tq, 2 * dv), F32)],
        compiler_params=_params("parallel", "arbitrary"),
        name="attn_prompt",
    )(lam, q, kt, vx, g)


def _attn_sample_kernel(pt_ref, lam_ref, q_ref, kn_ref, vn_ref, g_ref, *rest,
                        n_pg, n_heads, hd, post_scale):
    k_refs = rest[:n_pg]
    v_refs = rest[n_pg:2 * n_pg]
    o_ref, qp_ref, m_ref, l_ref, acc_ref = rest[2 * n_pg:]
    j = pl.program_id(1)
    dv = 2 * hd
    nr = q_ref.shape[0]
    page = k_refs[0].shape[0] // n_heads
    n_pairs = n_heads // 2
    nt_dims = (((1,), (1,)), ((), ()))

    @pl.when(j == 0)
    def _():
        for a in range(n_pairs):
            for side in range(2):
                h = 2 * a + side
                qh = q_ref[:, h * dv:(h + 1) * dv].astype(F32)
                row = lax.broadcasted_iota(jnp.int32, qh.shape, 0)
                lane = lax.broadcasted_iota(jnp.int32, qh.shape, 1)
                keep = (row < nr // 2) == (lane < hd)
                qm = jnp.where(keep, qh, 0.0).astype(BF)
                zero = jnp.zeros_like(qm)
                qp_ref[a, side * nr:(side + 1) * nr, :] = jnp.concatenate(
                    [qm, zero] if side == 0 else [zero, qm], axis=1)
        m_ref[...] = jnp.full(m_ref.shape, -jnp.inf, F32)
        l_ref[...] = jnp.zeros(l_ref.shape, F32)
        acc_ref[...] = jnp.zeros(acc_ref.shape, F32)

    def update(a, s, pv):
        m_prev = m_ref[a]
        m_new = jnp.maximum(m_prev, jnp.max(s, axis=1, keepdims=True))
        alpha = jnp.exp2(m_prev - m_new)
        p = jnp.exp2(s - m_new[:, :1])
        l_ref[a] = alpha * l_ref[a] + jnp.sum(p, axis=1, keepdims=True)
        acc_ref[a] = jnp.concatenate([alpha, alpha], axis=1) * acc_ref[a] + pv(p.astype(BF))
        m_ref[a] = m_new

    def pair_rows(page_refs, a):
        def head(h):
            return jnp.concatenate([r[pl.ds(h, page, stride=n_heads), :] for r in page_refs], axis=0)
        return jnp.concatenate([head(2 * a), head(2 * a + 1)], axis=1).astype(BF)

    for a in range(n_pairs):
        s = lax.dot_general(qp_ref[a], pair_rows(k_refs, a), nt_dims, preferred_element_type=F32)
        update(a, s, lambda p, a=a: jnp.dot(p, pair_rows(v_refs, a), preferred_element_type=F32))

    @pl.when(j == pl.num_programs(1) - 1)
    def _():
        for a in range(n_pairs):
            ps = slice(2 * a * dv, (2 * a + 2) * dv)
            s = lax.dot_general(qp_ref[a], kn_ref[:, ps], nt_dims, preferred_element_type=F32)
            row = lax.broadcasted_iota(jnp.int32, s.shape, 0)
            col = lax.broadcasted_iota(jnp.int32, s.shape, 1)
            s = jnp.where(col <= row % (nr // 2), s, NEG_BIG)
            update(a, s, lambda p, ps=ps: jnp.dot(p, vn_ref[:, ps], preferred_element_type=F32))
            for side in range(2):
                h = 2 * a + side
                t = (acc_ref[a, side * nr:(side + 1) * nr, side * dv:(side + 1) * dv]
                     / l_ref[a, side * nr:(side + 1) * nr, :])
                o = t[0:nr // 2] - lam_ref[0] * t[nr // 2:nr]
                ms = jnp.mean(o * o, axis=1, keepdims=True)
                o_ref[:, h * dv:(h + 1) * dv] = o * lax.rsqrt(ms + LN_EPS) * g_ref[...] * post_scale


def _attn_sample(page_table_flat, lam, q16, kn, vn, g, cache_k, cache_v, *,
                 layer, n_seq, n_pages, n_heads, hd, post_scale):
    n_pg = math.gcd(32, n_pages)
    nr = 2 * SUBLANES
    dv = 2 * hd
    w = n_heads * dv
    page_rows = cache_k.shape[2]
    page = page_rows // n_heads

    def kv_spec(g):
        return pl.BlockSpec((None, None, page_rows, dv),
                            lambda b, j, pt: (layer, pt[b * n_pages + j * n_pg + g], 0, 0))

    grid_spec = pltpu.PrefetchScalarGridSpec(
        num_scalar_prefetch=1,
        grid=(n_seq, n_pages // n_pg),
        in_specs=[pl.BlockSpec(memory_space=pltpu.SMEM),
                  pl.BlockSpec((nr, w), lambda b, j, pt: (b, 0)),
                  pl.BlockSpec((page, w), lambda b, j, pt: (b, 0)),
                  pl.BlockSpec((page, w), lambda b, j, pt: (b, 0)),
                  pl.BlockSpec((1, dv), lambda b, j, pt: (0, 0))]
                 + [kv_spec(g) for g in range(n_pg)] + [kv_spec(g) for g in range(n_pg)],
        out_specs=pl.BlockSpec((nr // 2, w), lambda b, j, pt: (b, 0)),
        scratch_shapes=[pltpu.VMEM((n_heads // 2, 2 * nr, 2 * dv), BF),
                        pltpu.VMEM((n_heads // 2, 2 * nr, dv), F32), pltpu.VMEM((n_heads // 2, 2 * nr, dv), F32),
                        pltpu.VMEM((n_heads // 2, 2 * nr, 2 * dv), F32)])
    return pl.pallas_call(
        functools.partial(_attn_sample_kernel, n_pg=n_pg, n_heads=n_heads, hd=hd, post_scale=post_scale),
        grid_spec=grid_spec,
        out_shape=jax.ShapeDtypeStruct((n_seq * (nr // 2), w), F32),
        compiler_params=_params("parallel", "arbitrary"),
        name="attn_sample",
    )(page_table_flat, lam, q16, kn, vn, g, *([cache_k] * n_pg), *([cache_v] * n_pg))


def _ssm_input_proj(ub, bw_ref, xre_ref, xim_ref):
    n_slab = bw_ref.shape[0]
    ns = xre_ref.shape[1] // n_slab
    for h in range(n_slab):
        uh = ub[:, h * MXU_K:(h + 1) * MXU_K]
        xre_ref[:, h * ns:(h + 1) * ns] = jnp.dot(uh, bw_ref[h, :, 0:ns], preferred_element_type=F32)
        xim_ref[:, h * ns:(h + 1) * ns] = jnp.dot(uh, bw_ref[h, :, ns:2 * ns], preferred_element_type=F32)


def _ssm_output(u, hre_ref, him_ref, cw_ref, d_ref, gw_ref, gb_ref):
    n_slab = cw_ref.shape[1]
    ns = hre_ref.shape[1] // n_slab
    ys = []
    for h in range(n_slab):
        ss = slice(h * ns, (h + 1) * ns)
        ys.append(jnp.dot(hre_ref[:, ss].astype(BF), cw_ref[0, h], preferred_element_type=F32)
                  + jnp.dot(him_ref[:, ss].astype(BF), cw_ref[1, h], preferred_element_type=F32))
    y = jnp.concatenate(ys, axis=1) + d_ref[...] * u
    a = jax.nn.gelu(y)
    z = jnp.dot(a.astype(BF), gw_ref[...], preferred_element_type=F32) + gb_ref[...]
    return a * jax.nn.sigmoid(z)


def _ssm_prompt_kernel(u_ref, bw_ref, cw_ref, pq_ref, d_ref, gw_ref, gb_ref,
                       out_ref, hre_out, him_out, utb_ref, xre_ref, xim_ref, cre_ref, cim_ref, *, lane_grp):
    @pl.when(pl.program_id(0) == 0)
    def _():
        cre_ref[...] = jnp.zeros(cre_ref.shape, F32)
        cim_ref[...] = jnp.zeros(cim_ref.shape, F32)

    n_seq, t_chunk = u_ref.shape[0], u_ref.shape[1]
    n_slab = utb_ref.shape[0]
    for b in range(n_seq):
        ub = u_ref[b].astype(F32)
        for c in range(n_slab):
            utb_ref[c, pl.ds(b, t_chunk, stride=n_seq), :] = ub[:, c * LANES:(c + 1) * LANES]
    u = jnp.concatenate([utb_ref[c] for c in range(n_slab)], axis=1)
    ub16 = u.astype(BF)

    rows, ns = xre_ref.shape
    half = SUBLANES // 2
    ns_slab = ns // bw_ref.shape[0]
    n_lg = ns // lane_grp

    def input_proj(lg):
        slab, off = divmod(lg * lane_grp, ns_slab)
        uh = ub16[:, slab * MXU_K:(slab + 1) * MXU_K]
        sl = slice(lg * lane_grp, (lg + 1) * lane_grp)
        xre_ref[:, sl] = jnp.dot(uh, bw_ref[slab, :, off:off + lane_grp], preferred_element_type=F32)
        xim_ref[:, sl] = jnp.dot(uh, bw_ref[slab, :, ns_slab + off:ns_slab + off + lane_grp],
                                 preferred_element_type=F32)

    input_proj(0)
    for lg in range(n_lg):
        if lg + 1 < n_lg:
            input_proj(lg + 1)
        sl = slice(lg * lane_grp, (lg + 1) * lane_grp)
        p_re, p_im, q_re, q_im = pq_ref[0, :, sl], pq_ref[1, :, sl], pq_ref[2, :, sl], pq_ref[3, :, sl]
        upper = lax.broadcasted_iota(jnp.int32, (SUBLANES, lane_grp), 0) >= half
        c_re, c_im = cre_ref[:, sl], cim_ref[:, sl]
        for i in range(rows // SUBLANES):
            rs = slice(i * SUBLANES, (i + 1) * SUBLANES)
            x_re = xre_ref[rs, sl]
            x_im = xim_ref[rs, sl]
            s_re = pltpu.roll(x_re, half, 0)
            s_im = pltpu.roll(x_im, half, 0)
            h_re = x_re + q_re * s_re - q_im * s_im + p_re * c_re - p_im * c_im
            h_im = x_im + q_re * s_im + q_im * s_re + p_re * c_im + p_im * c_re
            xre_ref[rs, sl] = h_re
            xim_ref[rs, sl] = h_im
            c_re = jnp.where(upper, h_re, pltpu.roll(h_re, half, 0))
            c_im = jnp.where(upper, h_im, pltpu.roll(h_im, half, 0))
        cre_ref[:, sl] = c_re
        cim_ref[:, sl] = c_im

    y = _ssm_output(u, xre_ref, xim_ref, cw_ref, d_ref, gw_ref, gb_ref)
    for c in range(n_slab):
        utb_ref[c] = y[:, c * LANES:(c + 1) * LANES]
    for b in range(n_seq):
        for c in range(n_slab):
            out_ref[b, :, c * LANES:(c + 1) * LANES] = utb_ref[c, pl.ds(b, t_chunk, stride=n_seq), :].astype(BF)
    hre_out[...] = cre_ref[...]
    him_out[...] = cim_ref[...]


def _ssm_prompt(su, bw, cw, pq, d, gw, gb, *, t_chunk):
    n_seq, seq, w = su.shape
    ns = pq.shape[2]
    rows = n_seq * t_chunk
    blk_map = lambda i: (0, i, 0)
    const2 = lambda i: (0, 0)
    return pl.pallas_call(
        functools.partial(_ssm_prompt_kernel, lane_grp=min(512, ns)),
        grid=(seq // t_chunk,),
        in_specs=[pl.BlockSpec((n_seq, t_chunk, w), blk_map),
                  pl.BlockSpec(bw.shape, lambda i: (0, 0, 0)),
                  pl.BlockSpec(cw.shape, lambda i: (0, 0, 0, 0)),
                  pl.BlockSpec(pq.shape, lambda i: (0, 0, 0)),
                  pl.BlockSpec((1, w), const2),
                  pl.BlockSpec(gw.shape, const2),
                  pl.BlockSpec((1, w), const2)],
        out_specs=(pl.BlockSpec((n_seq, t_chunk, w), blk_map),
                   pl.BlockSpec((SUBLANES, ns), const2), pl.BlockSpec((SUBLANES, ns), const2)),
        out_shape=(jax.ShapeDtypeStruct((n_seq, seq, w), BF),
                   jax.ShapeDtypeStruct((SUBLANES, ns), F32), jax.ShapeDtypeStruct((SUBLANES, ns), F32)),
        scratch_shapes=[pltpu.VMEM((w // LANES, rows, LANES), F32),
                        pltpu.VMEM((rows, ns), F32), pltpu.VMEM((rows, ns), F32),
                        pltpu.VMEM((SUBLANES, ns), F32), pltpu.VMEM((SUBLANES, ns), F32)],
        compiler_params=_params("arbitrary"),
        name="ssm_prompt",
    )(su, bw, cw, pq, d, gw, gb)


def _ssm_sample_kernel(u_ref, h0re_ref, h0im_ref, bw_ref, cw_ref, a_ref, d_ref, gw_ref, gb_ref,
                       out_ref, hre_out, him_out, xre_ref, xim_ref, *, n_steps):
    ub = u_ref[...]
    _ssm_input_proj(ub, bw_ref, xre_ref, xim_ref)
    nb = h0re_ref.shape[0]
    a_re, a_im = a_ref[0], a_ref[1]
    h_re, h_im = h0re_ref[...], h0im_ref[...]
    for t in range(n_steps):
        rs = slice(t * nb, (t + 1) * nb)
        h_re, h_im = (a_re * h_re - a_im * h_im + xre_ref[rs, :],
                      a_re * h_im + a_im * h_re + xim_ref[rs, :])
        xre_ref[rs, :] = h_re
        xim_ref[rs, :] = h_im
    out_ref[...] = _ssm_output(ub.astype(F32), xre_ref, xim_ref, cw_ref, d_ref, gw_ref, gb_ref).astype(BF)
    hre_out[...] = h_re
    him_out[...] = h_im


def _ssm_sample(su, h0_re, h0_im, bw, cw, a, d, gw, gb, *, n_steps):
    rows, w = su.shape
    nb, ns = h0_re.shape
    full = lambda arr: pl.BlockSpec(arr.shape, lambda i, n=arr.ndim: (0,) * n)
    return pl.pallas_call(
        functools.partial(_ssm_sample_kernel, n_steps=n_steps),
        grid=(1,),
        in_specs=[full(su), full(h0_re), full(h0_im), full(bw), full(cw), full(a), full(d), full(gw), full(gb)],
        out_specs=(pl.BlockSpec((rows, w), lambda i: (0, 0)),
                   pl.BlockSpec((nb, ns), lambda i: (0, 0)), pl.BlockSpec((nb, ns), lambda i: (0, 0))),
        out_shape=(jax.ShapeDtypeStruct((rows, w), BF),
                   jax.ShapeDtypeStruct((nb, ns), F32), jax.ShapeDtypeStruct((nb, ns), F32)),
        scratch_shapes=[pltpu.VMEM((rows, ns), F32), pltpu.VMEM((rows, ns), F32)],
        compiler_params=_params("arbitrary"),
        name="ssm_sample",
    )(su, h0_re, h0_im, bw, cw, a, d, gw, gb)


def _merge_kernel(at_ref, gm_ref, ss_ref, gates_ref, x_ref, wb_ref, wo_ref, lg_ref, lb_ref, *, alpha):
    d = x_ref.shape[1]
    merged = None
    for n, br in enumerate((at_ref, gm_ref, ss_ref)):
        proj = jnp.dot(br[...], wb_ref[n], preferred_element_type=F32)
        gated = gates_ref[:, n * d:(n + 1) * d].astype(F32) * proj
        merged = gated if merged is None else merged + gated
    mix = jnp.dot(merged.astype(BF), wo_ref[...], preferred_element_type=F32)
    return _layer_norm(alpha * x_ref[...] + mix, lg_ref[...], lb_ref[...])


def _merge_ffn_kernel(at_ref, gm_ref, ss_ref, gates_ref, x_ref, wb_ref, wo_ref, l1g_ref, l1b_ref,
                      wi_ref, wfo_ref, l2g_ref, l2b_ref, o_ref, hid_ref, *, alpha, tf):
    x1 = _merge_kernel(at_ref, gm_ref, ss_ref, gates_ref, x_ref, wb_ref, wo_ref, l1g_ref, l1b_ref,
                       alpha=alpha)
    xb = x1.astype(BF)
    d_ff = wfo_ref.shape[0]
    for c in range(d_ff // tf):
        gate = jnp.dot(xb, wi_ref[:, c * tf:(c + 1) * tf], preferred_element_type=F32)
        up = jnp.dot(xb, wi_ref[:, d_ff + c * tf:d_ff + (c + 1) * tf], preferred_element_type=F32)
        hid_ref[:, c * tf:(c + 1) * tf] = (gate * jax.nn.sigmoid(gate) * up).astype(BF)
    f = jnp.dot(hid_ref[...], wfo_ref[...], preferred_element_type=F32)
    o_ref[...] = _layer_norm(alpha * x1 + f, l2g_ref[...], l2b_ref[...])


def _merge_ffn(attn, gm, ssm, gates, x, wb_b, wo_b, l1g, l1b, w_in_b, w_out_b, l2g, l2b, *,
               tm, tf, alpha, layer):
    R, d = x.shape
    w = attn.shape[1]
    d_ff = w_out_b.shape[1]
    row_map = lambda i: (i, 0)
    const2 = lambda i: (0, 0)
    vec = lambda: pl.BlockSpec((1, d), const2)
    once = pl.Buffered(1)
    return pl.pallas_call(
        functools.partial(_merge_ffn_kernel, alpha=alpha, tf=tf),
        grid=(R // tm,),
        in_specs=[pl.BlockSpec((tm, w), row_map), pl.BlockSpec((tm, w), row_map),
                  pl.BlockSpec((tm, w), row_map),
                  pl.BlockSpec((tm, gates.shape[1]), row_map),
                  pl.BlockSpec((tm, d), row_map),
                  _layer_block(wb_b, layer, pipeline_mode=once), _layer_block(wo_b, layer, pipeline_mode=once),
                  vec(), vec(),
                  _layer_block(w_in_b, layer, pipeline_mode=once),
                  _layer_block(w_out_b, layer, pipeline_mode=once),
                  vec(), vec()],
        out_specs=pl.BlockSpec((tm, d), row_map),
        out_shape=jax.ShapeDtypeStruct((R, d), F32),
        scratch_shapes=[pltpu.VMEM((tm, d_ff), BF)],
        compiler_params=_params("parallel"),
        name="merge_ffn",
    )(attn, gm, ssm, gates, x, wb_b, wo_b, l1g, l1b, w_in_b, w_out_b, l2g, l2b)


def _rope_tables(pos, hd):
    half = hd // 2
    inv = ROPE_THETA ** (-jnp.arange(half, dtype=F32) / half)
    ang = pos.astype(F32)[:, None] * inv[None, :]
    cos, sin = jnp.cos(ang), jnp.sin(ang)
    reps = LANES // hd
    return (jnp.tile(jnp.concatenate([cos, cos], axis=1), (1, reps)),
            jnp.tile(jnp.concatenate([-sin, sin], axis=1), (1, reps)))


def _ssm_discretize(a_re, a_im, log_dt, b_re, b_im):
    dt = jnp.exp(log_dt.astype(F32))[:, None]
    ar, ai = a_re.astype(F32), a_im.astype(F32)
    mag = jnp.exp(dt * ar)
    abar_re = mag * jnp.cos(dt * ai)
    abar_im = mag * jnp.sin(dt * ai)
    nr, ni = abar_re - 1.0, abar_im
    den = ar * ar + ai * ai
    c_re = (nr * ar + ni * ai) / den
    c_im = (ni * ar - nr * ai) / den
    br, bi = b_re.astype(F32), b_im.astype(F32)
    bbar_re = c_re[..., None] * br - c_im[..., None] * bi
    bbar_im = c_re[..., None] * bi + c_im[..., None] * br
    return abar_re, abar_im, bbar_re, bbar_im


def _ssm_weights(a_re, a_im, log_dt, b_re, b_im, c_re, c_im):
    n_grp, n_state, grp_ch = b_re.shape
    abar_re, abar_im, bbar_re, bbar_im = _ssm_discretize(a_re, a_im, log_dt, b_re, b_im)
    gps = MXU_K // grp_ch
    n_slab = n_grp // gps
    eye = jnp.eye(gps, dtype=F32)

    def in_blocks(bb):
        bb = bb.reshape(n_slab, gps, n_state, grp_ch)
        return jnp.einsum('sgpc,gk->sgckp', bb, eye).reshape(n_slab, gps * grp_ch, gps * n_state)

    def out_blocks(cc):
        cc = cc.reshape(n_slab, gps, grp_ch, n_state)
        return jnp.einsum('sgcp,gk->sgpkc', cc, eye).reshape(n_slab, gps * n_state, gps * grp_ch)

    bw = jnp.concatenate([in_blocks(bbar_re), in_blocks(bbar_im)], axis=2).astype(BF)
    cw = jnp.stack([out_blocks(c_re.astype(F32)), -out_blocks(c_im.astype(F32))]).astype(BF)
    a_flat = jnp.stack([abar_re.reshape(1, -1), abar_im.reshape(1, -1)])
    a2_re = abar_re * abar_re - abar_im * abar_im
    a2_im = 2.0 * abar_re * abar_im
    half = SUBLANES // 2

    def halves(lo, hi):
        return jnp.concatenate([jnp.broadcast_to(lo.reshape(1, -1), (half, lo.size)),
                                jnp.broadcast_to(hi.reshape(1, -1), (half, hi.size))], axis=0)

    zero = jnp.zeros_like(abar_re)
    pq = jnp.stack([halves(abar_re, a2_re), halves(abar_im, a2_im),
                    halves(zero, abar_re), halves(zero, abar_im)])
    return bw, cw, a_flat, pq


def _lambda_init(l):
    return 0.8 - 0.6 * math.exp(-0.3 * l)


def kernel(x_prompt, x_sample, cache_k, cache_v, page_table, state_ssm_re, state_ssm_im, w_in, b_gate, lambda_q1, lambda_k1, lambda_q2, lambda_k2, attn_norm_g, gmlp_ln_g, gmlp_ln_b, gmlp_ws, gmlp_bs, ssm_a_re, ssm_a_im, ssm_log_dt, ssm_b_re, ssm_b_im, ssm_c_re, ssm_c_im, ssm_d, ssm_glu_w, ssm_glu_b, w_branch, w_out, ln1_g, ln1_b, ln2_g, ln2_b, w_ffn_in, w_ffn_out):
    bp, lp, d_model = x_prompt.shape
    bs, ls, _ = x_sample.shape
    depth = w_in.shape[0]
    n_pool, page, n_heads, dv = cache_k.shape[1:]
    hd = dv // 2
    n_pages = page_table.shape[1]
    n_past = n_pages * page
    wbr = ssm_glu_w.shape[1]
    n_grp_g, chunk = gmlp_ws.shape[1], gmlp_ws.shape[2]
    n_grp_s, n_state, grp_ch = ssm_b_re.shape[1:]
    ns = n_grp_s * n_state
    d_ff = w_ffn_out.shape[1]
    alpha = (2 * depth) ** 0.25
    rp, rs = bp * lp, bs * ls
    assert bp == SUBLANES // 2 and n_heads * dv == wbr and wbr % MXU_K == 0 and LANES % hd == 0
    assert dv == LANES and ls <= SUBLANES and rs == chunk and lp % page == 0

    xp = x_prompt.reshape(rp, d_model)
    xs = x_sample.transpose(1, 0, 2).reshape(rs, d_model)
    cos_p, sin_p = _rope_tables(jnp.arange(lp), hd)
    cos_s, sin_s = _rope_tables(jnp.repeat(n_past + jnp.arange(ls), bs), hd)
    ck = cache_k.reshape(depth, n_pool, page * n_heads, dv)
    cv = cache_v.reshape(depth, n_pool, page * n_heads, dv)
    pt_flat = page_table.reshape(-1).astype(jnp.int32)

    tm_p = min(512, lp)
    tk = min(512, lp)
    tf = 256 if d_ff % 256 == 0 else d_ff
    t_chunk = min(256, lp)
    gc = wbr // n_grp_g

    w_in_b, wb_b, wo_b = w_in.astype(BF), w_branch.astype(BF), w_out.astype(BF)
    wfi_b, wfo_b = w_ffn_in.astype(BF), w_ffn_out.astype(BF)

    outs = {k: [] for k in ("ks", "vs", "hrp", "hip", "hrs", "his", "gvs")}
    kv_stacked = ()
    for l in range(depth):
        lam_init = _lambda_init(l)
        post_scale = 1.0 - lam_init
        lam = (jnp.exp(jnp.sum(lambda_q1[l].astype(F32) * lambda_k1[l].astype(F32)))
               - jnp.exp(jnp.sum(lambda_q2[l].astype(F32) * lambda_k2[l].astype(F32)))
               + lam_init).reshape(1)
        glu_b16 = ssm_glu_w[l].astype(BF)
        an_g = attn_norm_g[l].reshape(1, dv)
        ws_tril = jnp.tril(gmlp_ws[l])
        bw, cw, a_flat, pq = _ssm_weights(ssm_a_re[l], ssm_a_im[l], ssm_log_dt[l], ssm_b_re[l], ssm_b_im[l],
                                          ssm_c_re[l], ssm_c_im[l])
        d_row = ssm_d[l].reshape(1, wbr)
        glu_bias = ssm_glu_b[l].reshape(1, wbr)
        ln_gg, ln_gb = gmlp_ln_g[l].reshape(1, wbr), gmlp_ln_b[l].reshape(1, wbr)
        bgate = b_gate[l].reshape(1, -1)
        l1g, l1b = ln1_g[l].reshape(1, d_model), ln1_b[l].reshape(1, d_model)
        l2g, l2b = ln2_g[l].reshape(1, d_model), ln2_b[l].reshape(1, d_model)

        bias_p = jnp.repeat(gmlp_bs[l].T, gc, axis=1)
        q, kt, kf_all, vx, vf_all, gm, su, gates = _in_proj(
            xp, w_in_b, cos_p, sin_p, bgate, ln_gg, ln_gb, ws_tril.astype(BF), bias_p, kv_stacked,
            wb=wbr, hd=hd, n_heads=n_heads, tm=tm_p, tk=tk, rows_per_seq=lp, prompt=True,
            layer=l, depth=depth)
        kv_stacked = (kf_all, vf_all)
        attn = _attn_prompt(lam, q, kt, vx, an_g, n_seq=bp, seq=lp, n_heads=n_heads, hd=hd, tk=tk,
                            post_scale=post_scale)
        ssm, hre, him = _ssm_prompt(su.reshape(bp, lp, wbr), bw, cw, pq, d_row, glu_b16, glu_bias,
                                    t_chunk=t_chunk)
        xp = _merge_ffn(attn, gm, ssm.reshape(rp, wbr), gates, xp, wb_b, wo_b, l1g, l1b,
                        wfi_b, wfo_b, l2g, l2b, tm=min(512, rp), tf=tf, alpha=alpha, layer=l)
        outs["hrp"].append(hre[SUBLANES // 2:].reshape(bp, n_grp_s, n_state))
        outs["hip"].append(him[SUBLANES // 2:].reshape(bp, n_grp_s, n_state))

        ws_s = jnp.einsum('gts,bc->gtbsc', ws_tril[:, :ls, :ls], jnp.eye(bs, dtype=F32)).reshape(n_grp_g, rs, rs)
        bias_s = jnp.repeat(jnp.repeat(gmlp_bs[l][:, :ls].T, bs, axis=0), gc, axis=1)
        q, kb, kf, vb, vf, gm, vn, su, gates = _in_proj(
            xs, w_in_b, cos_s, sin_s, bgate, ln_gg, ln_gb, ws_s.astype(BF), bias_s,
            wb=wbr, hd=hd, n_heads=n_heads, tm=rs, tk=rs, rows_per_seq=rs, prompt=False, layer=l)
        q3 = q.reshape(ls, bs, wbr).transpose(1, 0, 2)
        zq = jnp.zeros((bs, SUBLANES - ls, wbr), BF)
        q16 = jnp.concatenate([q3, zq, q3, zq], axis=1).reshape(bs * 2 * SUBLANES, wbr)
        pad_new = lambda a: jnp.pad(a.reshape(ls, bs, wbr).transpose(1, 0, 2),
                                    ((0, 0), (0, page - ls), (0, 0))).reshape(bs * page, wbr)
        attn8 = _attn_sample(pt_flat, lam, q16, pad_new(kb), pad_new(vb), an_g, ck, cv,
                             layer=l, n_seq=bs, n_pages=n_pages, n_heads=n_heads, hd=hd,
                             post_scale=post_scale)
        attn = attn8.reshape(bs, SUBLANES, wbr)[:, :ls].transpose(1, 0, 2).reshape(rs, wbr).astype(BF)
        ssm, hre, him = _ssm_sample(su, state_ssm_re[l].reshape(bs, ns), state_ssm_im[l].reshape(bs, ns),
                                    bw, cw, a_flat, d_row, glu_b16, glu_bias, n_steps=ls)
        xs = _merge_ffn(attn, gm, ssm, gates, xs, wb_b, wo_b, l1g, l1b,
                        wfi_b, wfo_b, l2g, l2b, tm=rs, tf=tf, alpha=alpha, layer=l)
        to_seq_major = lambda a: a.reshape(ls, bs, -1).transpose(1, 0, 2)
        outs["ks"].append(to_seq_major(kf).reshape(bs, ls, n_heads, dv))
        outs["vs"].append(to_seq_major(vf).reshape(bs, ls, n_heads, dv))
        outs["hrs"].append(hre.reshape(bs, n_grp_s, n_state))
        outs["his"].append(him.reshape(bs, n_grp_s, n_state))
        outs["gvs"].append(to_seq_major(vn))

    st = lambda k: jnp.stack(outs[k])
    kv_shape = (depth, bp, lp // page, page, n_heads, dv)
    return (xp.reshape(bp, lp, d_model), xs.reshape(ls, bs, d_model).transpose(1, 0, 2),
            kv_stacked[0].reshape(kv_shape), kv_stacked[1].reshape(kv_shape), st("ks"), st("vs"),
            st("hrp"), st("hip"), st("hrs"), st("his"), st("gvs"))
```
